```python
import math
import jax, jax.numpy as jnp
from jax import lax
import numpy as np

D_MODEL = 2048
BATCH = 4
SEQ = 4096
DEPTH = 2

N_EVEN = (DEPTH + 1) // 2
N_ODD = DEPTH // 2
EPS = 1e-6

DA_WIDTH = D_MODEL // 2
DA_HEADS = 8
DA_HEAD_DIM = DA_WIDTH // DA_HEADS // 2
DA_V_DIM = 2 * DA_HEAD_DIM
ROPE_DIM = DA_HEAD_DIM // 4
ROPE_THETA = 500000.0
Q_BLOCK = 128

RW_WIDTH = D_MODEL // 2
RW_HEAD = 64
RW_HEADS = RW_WIDTH // RW_HEAD
DECAY_LORA = 64
AAA_LORA = 64
GATE_LORA = 32
RW_IN = 3 * RW_WIDTH + DECAY_LORA + AAA_LORA + GATE_LORA
RW_GN_EPS = 64e-5

EVEN_IN = 3 * DA_WIDTH + RW_IN
EVEN_OUT = DA_WIDTH + RW_WIDTH

HG_EXPAND = 128
HG_HEADS = D_MODEL // HG_EXPAND
HG_HEAD_DIM = D_MODEL // HG_HEADS
HG_CHUNK = 64
ODD_IN = 4 * D_MODEL

FFN_HIDDEN = -(-8 * D_MODEL // (3 * 256)) * 256

kernel_name = "hybrid_diffattn_rwkv7_hgrn2_swiglu"


def rms_norm(x, g, eps=EPS):
    xf = x.astype(jnp.float32)
    y = xf * lax.rsqrt(jnp.mean(xf * xf, axis=-1, keepdims=True) + eps)
    return (y * g).astype(x.dtype)


def partial_rope(x, cos, sin):
    half = ROPE_DIM // 2
    x1, x2, xp = x[..., :half], x[..., half:ROPE_DIM], x[..., ROPE_DIM:]
    return jnp.concatenate([x1 * cos - x2 * sin, x2 * cos + x1 * sin, xp], axis=-1).astype(x.dtype)


def diff_attention(qa, ka, va, lam_vec, subln_g, lam_init):
    B, S, _ = qa.shape
    pos = jnp.arange(S, dtype=jnp.float32)
    inv_freq = 1.0 / (ROPE_THETA ** (jnp.arange(0, ROPE_DIM, 2, dtype=jnp.float32) / ROPE_DIM))
    ang = pos[:, None] * inv_freq[None, :]
    cos = jnp.cos(ang)[:, None, None, :]
    sin = jnp.sin(ang)[:, None, None, :]
    q = partial_rope(qa.reshape(B, S, DA_HEADS, 2, DA_HEAD_DIM), cos, sin).transpose(0, 2, 3, 1, 4)
    k = partial_rope(ka.reshape(B, S, DA_HEADS, 2, DA_HEAD_DIM), cos, sin).transpose(0, 2, 3, 1, 4)
    v = va.reshape(B, S, DA_HEADS, DA_V_DIM).transpose(0, 2, 1, 3)
    lf = lam_vec.astype(jnp.float32)
    lam = jnp.exp(jnp.sum(lf[0] * lf[1])) - jnp.exp(jnp.sum(lf[2] * lf[3])) + lam_init
    scale = DA_HEAD_DIM ** -0.5
    outs = []
    for blk in range(S // Q_BLOCK):
        q0 = blk * Q_BLOCK
        kend = q0 + Q_BLOCK
        s = jnp.einsum('bhcqd,bhckd->bhcqk', q[:, :, :, q0:kend], k[:, :, :, :kend]).astype(jnp.float32) * scale
        mask = jnp.arange(kend)[None, :] <= (q0 + jnp.arange(Q_BLOCK))[:, None]
        p = jax.nn.softmax(jnp.where(mask, s, -jnp.inf), axis=-1)
        w = p[:, :, 0] - lam * p[:, :, 1]
        outs.append(jnp.einsum('bhqk,bhkv->bhqv', w.astype(v.dtype), v[:, :, :kend]))
    o = jnp.concatenate(outs, axis=2)
    o = rms_norm(o, subln_g) * (1.0 - lam_init)
    return o.transpose(0, 2, 1, 3).reshape(B, S, DA_WIDTH)


def rwkv7_time_mix(p, mu, w0, w_up, a0, a_up, g_up, k_k, k_a, r_k, ln_w, ln_b):
    B, S, _ = p.shape
    prev = jnp.pad(p, ((0, 0), (1, 0), (0, 0)))[:, :-1]
    xm = p + (prev - p) * mu
    c = [RW_WIDTH, 2 * RW_WIDTH, 3 * RW_WIDTH, 3 * RW_WIDTH + DECAY_LORA, 3 * RW_WIDTH + DECAY_LORA + AAA_LORA]
    r, k, v, wl, al, gl = jnp.split(xm, c, axis=-1)
    w_log = -jax.nn.softplus(-(w0 + jnp.tanh(wl) @ w_up)) - 0.5
    decay = jnp.exp(-jnp.exp(w_log.astype(jnp.float32)))
    a = jax.nn.sigmoid(a0 + al @ a_up)
    g = jax.nn.sigmoid(gl) @ g_up
    heads = lambda t: t.reshape(B, S, RW_HEADS, RW_HEAD)
    kk = heads(k * k_k).astype(jnp.float32)
    kk = kk / jnp.maximum(jnp.linalg.norm(kk, axis=-1, keepdims=True), 1e-12)
    k = k * (1.0 + (a - 1.0) * k_a)
    r_h, k_h, v_h, a_h, w_h = heads(r), heads(k), heads(v), heads(a), heads(decay)
    xs = tuple(t.astype(jnp.float32).transpose(1, 0, 2, 3) for t in (r_h, w_h, k_h, v_h, kk, a_h))

    def step(state, inp):
        rt, wt, kt, vt, kkt, at = inp
        sa = jnp.einsum('bhvk,bhk->bhv', state, -kkt)
        state = state * wt[:, :, None, :] + sa[..., None] * (kkt * at)[:, :, None, :] + vt[..., None] * kt[:, :, None, :]
        return state, jnp.einsum('bhvk,bhk->bhv', state, rt)

    init = jnp.zeros((B, RW_HEADS, RW_HEAD, RW_HEAD), jnp.float32)
    _, y = lax.scan(step, init, xs)
    y = y.transpose(1, 0, 2, 3)
    mean = jnp.mean(y, axis=-1, keepdims=True)
    var = jnp.mean(jnp.square(y - mean), axis=-1, keepdims=True)
    yn = ((y - mean) * lax.rsqrt(var + RW_GN_EPS)).reshape(B, S, RW_WIDTH) * ln_w + ln_b
    bonus = (jnp.sum(r_h * k_h * r_k, axis=-1, keepdims=True) * v_h).reshape(B, S, RW_WIDTH)
    return ((yn + bonus) * g).astype(p.dtype)


def hgrn2(p, lb, norm_g):
    B, S, _ = p.shape
    q, f, i, g = jnp.split(p, 4, axis=-1)
    f = lb + (1.0 - lb) * jax.nn.sigmoid(f.astype(jnp.float32))
    log_f = jnp.log(f)
    k = 1.0 - f
    n = S // HG_CHUNK
    chunks = lambda t: t.astype(jnp.float32).reshape(B, n, HG_CHUNK, HG_HEADS, -1).transpose(1, 0, 3, 2, 4)
    mask = jnp.tril(jnp.ones((HG_CHUNK, HG_CHUNK), bool))

    def step(state, inp):
        qc, kc, ic, lfc = inp
        b = jnp.cumsum(lfc, axis=-2)
        o_inter = jnp.einsum('bhtd,bhdv->bhtv', qc * jnp.exp(b), state)
        decay = jnp.exp(jnp.where(mask[:, :, None], b[:, :, :, None, :] - b[:, :, None, :, :], -jnp.inf))
        attn = jnp.einsum('bhtd,bhsd,bhtsd->bhts', qc, kc, decay)
        o = o_inter + jnp.einsum('bhts,bhsv->bhtv', attn, ic)
        b_last = b[:, :, -1:, :]
        state = jnp.exp(b_last[:, :, 0, :])[..., None] * state + jnp.einsum('bhsd,bhsv->bhdv', kc * jnp.exp(b_last - b), ic)
        return state, o

    init = jnp.zeros((B, HG_HEADS, HG_EXPAND, HG_HEAD_DIM), jnp.float32)
    _, o = lax.scan(step, init, (chunks(q), chunks(k), chunks(i), chunks(log_f)))
    o = o.transpose(1, 0, 3, 2, 4).reshape(B, S, HG_HEADS, HG_HEAD_DIM)
    o = rms_norm(o, norm_g).reshape(B, S, D_MODEL) * jax.nn.silu(g)
    return o.astype(p.dtype)


def setup_inputs(seed: int = 0) -> dict:
    key = jax.random.key(seed)
    ks = iter(list(jax.random.split(key, 32)))
    nrm = lambda shape, scale: jax.random.normal(next(ks), shape, jnp.float32) * scale
    gain = lambda shape: 1.0 + nrm(shape, 0.02)
    D = D_MODEL
    return {
        "x": nrm((BATCH, SEQ, D), 1.0),
        "attn_norm_g": gain((DEPTH, D)),
        "ffn_norm_g": gain((DEPTH, D)),
        "final_norm_g": gain((D,)),
        "even_w_in": nrm((N_EVEN, D, EVEN_IN), D ** -0.5),
        "even_w_out": nrm((N_EVEN, EVEN_OUT, D), EVEN_OUT ** -0.5),
        "da_lambda": nrm((N_EVEN, 4, DA_HEAD_DIM), 0.1),
        "da_subln_g": gain((N_EVEN, DA_V_DIM)),
        "rw_mu": jax.random.uniform(next(ks), (N_EVEN, RW_IN), jnp.float32),
        "rw_w0": jax.random.uniform(next(ks), (N_EVEN, RW_WIDTH), jnp.float32, minval=-6.5, maxval=-1.5),
        "rw_w_up": nrm((N_EVEN, DECAY_LORA, RW_WIDTH), 0.1 * DECAY_LORA ** -0.5),
        "rw_a0": nrm((N_EVEN, RW_WIDTH), 0.1),
        "rw_a_up": nrm((N_EVEN, AAA_LORA, RW_WIDTH), 0.5 * AAA_LORA ** -0.5),
        "rw_g_up": nrm((N_EVEN, GATE_LORA, RW_WIDTH), GATE_LORA ** -0.5),
        "rw_k_k": 0.85 + nrm((N_EVEN, RW_WIDTH), 0.02),
        "rw_k_a": gain((N_EVEN, RW_WIDTH)),
        "rw_r_k": nrm((N_EVEN, RW_HEADS, RW_HEAD), 0.1),
        "rw_ln_w": gain((N_EVEN, RW_WIDTH)),
        "rw_ln_b": nrm((N_EVEN, RW_WIDTH), 0.01),
        "odd_w_in": nrm((N_ODD, D, ODD_IN), D ** -0.5),
        "odd_w_out": nrm((N_ODD, D, D), D ** -0.5),
        "hg_lower_bound": nrm((DEPTH, D), 1.0),
        "hg_norm_g": gain((N_ODD, HG_HEAD_DIM)),
        "ffn_w_gate": nrm((DEPTH, D, FFN_HIDDEN), D ** -0.5),
        "ffn_w_up": nrm((DEPTH, D, FFN_HIDDEN), D ** -0.5),
        "ffn_w_down": nrm((DEPTH, FFN_HIDDEN, D), FFN_HIDDEN ** -0.5),
    }


def reference(x, attn_norm_g, ffn_norm_g, final_norm_g, even_w_in, even_w_out, da_lambda, da_subln_g,
              rw_mu, rw_w0, rw_w_up, rw_a0, rw_a_up, rw_g_up, rw_k_k, rw_k_a, rw_r_k, rw_ln_w, rw_ln_b,
              odd_w_in, odd_w_out, hg_lower_bound, hg_norm_g, ffn_w_gate, ffn_w_up, ffn_w_down):
    s_lb = jax.nn.softmax(hg_lower_bound.astype(jnp.float32), axis=0)
    lb_all = jnp.cumsum(s_lb, axis=0) - s_lb[0]
    for layer in range(DEPTH):
        j = layer // 2
        h = rms_norm(x, attn_norm_g[layer])
        if layer % 2 == 0:
            p = h @ even_w_in[j]
            qa, ka, va = jnp.split(p[..., :3 * DA_WIDTH], 3, axis=-1)
            lam_init = 0.8 - 0.6 * math.exp(-0.3 * layer)
            a_out = diff_attention(qa, ka, va, da_lambda[j], da_subln_g[j], lam_init)
            b_out = rwkv7_time_mix(p[..., 3 * DA_WIDTH:], rw_mu[j], rw_w0[j], rw_w_up[j], rw_a0[j], rw_a_up[j],
                                   rw_g_up[j], rw_k_k[j], rw_k_a[j], rw_r_k[j], rw_ln_w[j], rw_ln_b[j])
            mix = jnp.concatenate([a_out, b_out.astype(a_out.dtype)], axis=-1) @ even_w_out[j]
        else:
            p = h @ odd_w_in[j]
            mix = hgrn2(p, lb_all[layer], hg_norm_g[j]) @ odd_w_out[j]
        x = x + mix.astype(x.dtype)
        h = rms_norm(x, ffn_norm_g[layer])
        x = x + ((jax.nn.silu(h @ ffn_w_gate[layer]) * (h @ ffn_w_up[layer])) @ ffn_w_down[layer]).astype(x.dtype)
    return rms_norm(x, final_norm_g)
```

```python
import functools
import math

import jax
import jax.numpy as jnp
from jax import lax
from jax.experimental import pallas as pl
from jax.experimental.pallas import tpu as pltpu

D_MODEL = 2048
EPS = 1e-6

DA_WIDTH = D_MODEL // 2
DA_HEADS = 8
DA_HEAD_DIM = DA_WIDTH // DA_HEADS // 2
DA_V_DIM = 2 * DA_HEAD_DIM
ROPE_DIM = DA_HEAD_DIM // 4
ROPE_THETA = 500000.0

RW_WIDTH = D_MODEL // 2
RW_HEAD = 64
RW_HEADS = RW_WIDTH // RW_HEAD
DECAY_LORA = 64
AAA_LORA = 64
GATE_LORA = 32
RW_LORA = DECAY_LORA + AAA_LORA + GATE_LORA
RW_LORA_PAD = 256
RW_GN_EPS = 64e-5
RW_CHUNK = 64

HG_EXPAND = 128
HG_HEADS = D_MODEL // HG_EXPAND
HG_CHUNK = 64
HG_SUB = 16

LANES = 128
NEG_BIG = -1e30
VMEM_LIMIT = 48 * 1024 * 1024

F32 = jnp.float32
BF16 = jnp.bfloat16
HIGHEST = lax.Precision.HIGHEST

_NN = (((1,), (0,)), ((), ()))
_NT = (((1,), (1,)), ((), ()))
_TN = (((0,), (0,)), ((), ()))


def _dot_exact(a, b, dims=_NN):
    return lax.dot_general(a, b, dims, precision=HIGHEST, preferred_element_type=F32)


def _dot_bf16(a, b, dims=_NN):
    return lax.dot_general(a.astype(BF16), b.astype(BF16), dims, preferred_element_type=F32)


def _sigmoid(x):
    return 1.0 / (1.0 + jnp.exp(-x))


def _params(*sem):
    return pltpu.CompilerParams(dimension_semantics=sem, vmem_limit_bytes=VMEM_LIMIT)


def _rmsnorm_body(x_ref, g_ref, o_ref):
    x = x_ref[...]
    ms = jnp.mean(x * x, axis=-1, keepdims=True)
    o_ref[...] = (x * lax.rsqrt(ms + EPS) * g_ref[...]).astype(o_ref.dtype)


def _rmsnorm(x2d, g, out_dtype, tm=512):
    t, d = x2d.shape
    tm = min(tm, t)
    return pl.pallas_call(
        _rmsnorm_body,
        grid=(t // tm,),
        in_specs=[pl.BlockSpec((tm, d), lambda i: (i, 0)), pl.BlockSpec((1, d), lambda i: (0, 0))],
        out_specs=pl.BlockSpec((tm, d), lambda i: (i, 0)),
        out_shape=jax.ShapeDtypeStruct((t, d), out_dtype),
        compiler_params=_params("parallel"),
        name="rmsnorm",
    )(x2d, g.reshape(1, d))


def _mm_body(a_ref, w_ref, o_ref):
    o_ref[...] = jnp.dot(a_ref[...], w_ref[...], preferred_element_type=F32).astype(o_ref.dtype)


def _matmul(a, w, out_dtype, tm=1024, tn=512):
    m, k = a.shape
    n = w.shape[1]
    tm, tn = min(tm, m), min(tn, n)
    return pl.pallas_call(
        _mm_body,
        grid=(m // tm, n // tn),
        in_specs=[pl.BlockSpec((tm, k), lambda i, j: (i, 0)), pl.BlockSpec((k, tn), lambda i, j: (0, j))],
        out_specs=pl.BlockSpec((tm, tn), lambda i, j: (i, j)),
        out_shape=jax.ShapeDtypeStruct((m, n), out_dtype),
        compiler_params=_params("parallel", "arbitrary"),
        name="matmul",
    )(a, w)


def _mm_res_body(a_ref, w_ref, r_ref, o_ref):
    o_ref[...] = r_ref[...] + jnp.dot(a_ref[...], w_ref[...], preferred_element_type=F32)


def _matmul_residual(a, w, res, tm=512, tn=512):
    m, k = a.shape
    n = w.shape[1]
    tm, tn = min(tm, m), min(tn, n)
    return pl.pallas_call(
        _mm_res_body,
        grid=(m // tm, n // tn),
        in_specs=[
            pl.BlockSpec((tm, k), lambda i, j: (i, 0)),
            pl.BlockSpec((k, tn), lambda i, j: (0, j)),
            pl.BlockSpec((tm, tn), lambda i, j: (i, j)),
        ],
        out_specs=pl.BlockSpec((tm, tn), lambda i, j: (i, j)),
        out_shape=jax.ShapeDtypeStruct((m, n), F32),
        compiler_params=_params("parallel", "arbitrary"),
        name="matmul_residual",
    )(a, w, res)


def _mm2_res_body(a1_ref, a2_ref, w1_ref, w2_ref, r_ref, o_ref):
    acc = jnp.dot(a1_ref[...], w1_ref[...], preferred_element_type=F32)
    acc = acc + jnp.dot(a2_ref[...], w2_ref[...], preferred_element_type=F32)
    o_ref[...] = r_ref[...] + acc


def _matmul2_residual(a1, a2, w1, w2, res, tm=1024, tn=512):
    m, k1 = a1.shape
    k2 = a2.shape[1]
    n = w1.shape[1]
    tm, tn = min(tm, m), min(tn, n)
    return pl.pallas_call(
        _mm2_res_body,
        grid=(m // tm, n // tn),
        in_specs=[
            pl.BlockSpec((tm, k1), lambda i, j: (i, 0)),
            pl.BlockSpec((tm, k2), lambda i, j: (i, 0)),
            pl.BlockSpec((k1, tn), lambda i, j: (0, j)),
            pl.BlockSpec((k2, tn), lambda i, j: (0, j)),
            pl.BlockSpec((tm, tn), lambda i, j: (i, j)),
        ],
        out_specs=pl.BlockSpec((tm, tn), lambda i, j: (i, j)),
        out_shape=jax.ShapeDtypeStruct((m, n), F32),
        compiler_params=_params("parallel", "arbitrary"),
        name="out_proj_residual",
    )(a1, a2, w1, w2, res)


def _ffn_up_body(h_ref, wg_ref, wu_ref, o_ref):
    h = h_ref[...]
    g = jnp.dot(h, wg_ref[...], preferred_element_type=F32)
    u = jnp.dot(h, wu_ref[...], preferred_element_type=F32)
    o_ref[...] = (g * _sigmoid(g) * u).astype(o_ref.dtype)


def _ffn_up(h, wg, wu, tm=1024, tn=512):
    m, k = h.shape
    n = wg.shape[1]
    tm, tn = min(tm, m), min(tn, n)
    return pl.pallas_call(
        _ffn_up_body,
        grid=(m // tm, n // tn),
        in_specs=[
            pl.BlockSpec((tm, k), lambda i, j: (i, 0)),
            pl.BlockSpec((k, tn), lambda i, j: (0, j)),
            pl.BlockSpec((k, tn), lambda i, j: (0, j)),
        ],
        out_specs=pl.BlockSpec((tm, tn), lambda i, j: (i, j)),
        out_shape=jax.ShapeDtypeStruct((m, n), BF16),
        compiler_params=_params("parallel", "arbitrary"),
        name="ffn_up",
    )(h, wg, wu)


def _qk_rope_body(h_ref, w_ref, cos_ref, sin_lo_ref, sin_hi_ref, o_ref, *, n_q_tiles, tn, q_scale):
    acc = jnp.dot(h_ref[...], w_ref[...], preferred_element_type=F32)
    scale = jnp.where(pl.program_id(1) < n_q_tiles, q_scale, 1.0).astype(F32)
    cos = cos_ref[...]
    sin_lo = sin_lo_ref[...]
    sin_hi = sin_hi_ref[...]
    half = ROPE_DIM // 2
    for g in range(tn // LANES):
        x = acc[:, g * LANES:(g + 1) * LANES]
        y = x * cos + pltpu.roll(x, half, 1) * sin_hi + pltpu.roll(x, LANES - half, 1) * sin_lo
        o_ref[:, g * LANES:(g + 1) * LANES] = (y * scale).astype(o_ref.dtype)


def _rope_tables(seq):
    pos = jnp.arange(seq, dtype=F32)
    inv_freq = 1.0 / (ROPE_THETA ** (jnp.arange(0, ROPE_DIM, 2, dtype=F32) / ROPE_DIM))
    ang = pos[:, None] * inv_freq[None, :]
    cos, sin = jnp.cos(ang), jnp.sin(ang)
    half = ROPE_DIM // 2
    pad = jnp.zeros((seq, DA_HEAD_DIM - ROPE_DIM), F32)
    zer = jnp.zeros((seq, half), F32)
    cos_c = jnp.concatenate([cos, cos, pad + 1.0], axis=1)
    sin_lo = jnp.concatenate([-sin, zer, pad], axis=1)
    sin_hi = jnp.concatenate([zer, sin, pad], axis=1)
    rep = lambda t: jnp.concatenate([t, t], axis=1)
    return rep(cos_c), rep(sin_lo), rep(sin_hi)


def _qk_proj(h, w_qk, seq, tm=1024, tn=512):
    m, k = h.shape
    n = w_qk.shape[1]
    tm, tn = min(tm, seq), min(tn, n)
    cos, sin_lo, sin_hi = _rope_tables(seq)
    s_tiles = seq // tm
    body = functools.partial(_qk_rope_body, n_q_tiles=DA_WIDTH // tn, tn=tn, q_scale=DA_HEAD_DIM ** -0.5)
    tab = pl.BlockSpec((tm, LANES), lambda i, j: (i % s_tiles, 0))
    return pl.pallas_call(
        body,
        grid=(m // tm, n // tn),
        in_specs=[pl.BlockSpec((tm, k), lambda i, j: (i, 0)), pl.BlockSpec((k, tn), lambda i, j: (0, j)),
                  tab, tab, tab],
        out_specs=pl.BlockSpec((tm, tn), lambda i, j: (i, j)),
        out_shape=jax.ShapeDtypeStruct((m, n), BF16),
        compiler_params=_params("parallel", "arbitrary"),
        name="qk_proj_rope",
    )(h, w_qk, cos, sin_lo, sin_hi)


def _attn_body(lam_ref, g_ref, q_ref, k_ref, v_ref, o_ref, *, tq, lam_init):
    qi = pl.program_id(2)
    q = q_ref[...]
    lane = lax.broadcasted_iota(jnp.int32, (1, LANES), 1)
    zero = jnp.zeros_like(q)
    qs = jnp.concatenate([jnp.where(lane < DA_HEAD_DIM, q, zero), jnp.where(lane >= DA_HEAD_DIM, q, zero)], axis=0)

    def step(j, carry, masked):
        m, l, acc = carry
        off = pl.multiple_of(j * tq, tq)
        k = k_ref[pl.ds(off, tq), :]
        v = v_ref[pl.ds(off, tq), :]
        s = lax.dot_general(qs, k, _NT, preferred_element_type=F32)
        if masked:
            r = lax.broadcasted_iota(jnp.int32, (2 * tq, tq), 0)
            r = jnp.where(r >= tq, r - tq, r)
            c = lax.broadcasted_iota(jnp.int32, (2 * tq, tq), 1)
            s = jnp.where(c <= r, s, NEG_BIG)
        m_new = jnp.maximum(m, jnp.max(s, axis=-1, keepdims=True))
        alpha = jnp.exp(m - m_new)
        p = jnp.exp(s - m_new)
        l = alpha * l + jnp.sum(p, axis=-1, keepdims=True)
        acc = alpha * acc + jnp.dot(p.astype(BF16), v, preferred_element_type=F32)
        return m_new, l, acc

    init = (jnp.full((2 * tq, 1), NEG_BIG, F32), jnp.zeros((2 * tq, 1), F32), jnp.zeros((2 * tq, LANES), F32))
    carry = lax.fori_loop(0, qi, lambda j, c: step(j, c, False), init)
    _, l, acc = step(qi, carry, True)

    lf = lam_ref[...]
    lam = (jnp.exp(jnp.sum(lf[0:1] * lf[1:2], axis=-1, keepdims=True))
           - jnp.exp(jnp.sum(lf[2:3] * lf[3:4], axis=-1, keepdims=True)) + lam_init)
    o = acc[:tq] / l[:tq] - lam * (acc[tq:] / l[tq:])
    o = o * lax.rsqrt(jnp.mean(o * o, axis=-1, keepdims=True) + EPS) * g_ref[...]
    o_ref[...] = (o * (1.0 - lam_init)).astype(o_ref.dtype)


def _diff_attention(qk, v, lam_vec, subln_g, batch, seq, lam_init, tq=256):
    tq = min(tq, seq)
    nq = seq // tq
    body = functools.partial(_attn_body, tq=tq, lam_init=lam_init)
    return pl.pallas_call(
        body,
        grid=(batch, DA_HEADS, nq),
        in_specs=[
            pl.BlockSpec((4, DA_HEAD_DIM), lambda b, h, i: (0, 0)),
            pl.BlockSpec((1, DA_V_DIM), lambda b, h, i: (0, 0)),
            pl.BlockSpec((tq, LANES), lambda b, h, i: (b * nq + i, h)),
            pl.BlockSpec((seq, LANES), lambda b, h, i: (b, DA_HEADS + h)),
            pl.BlockSpec((seq, LANES), lambda b, h, i: (b, h)),
        ],
        out_specs=pl.BlockSpec((tq, LANES), lambda b, h, i: (b * nq + i, h)),
        out_shape=jax.ShapeDtypeStruct((batch * seq, DA_WIDTH), BF16),
        compiler_params=_params("parallel", "parallel", "arbitrary"),
        name="diff_attention",
    )(lam_vec, subln_g.reshape(1, DA_V_DIM), qk, qk, v)


def _rwkv_prep_body(x_ref, xprev_ref, l_ref, lprev_ref, mu_ref, mul_ref, w0_ref, a0_ref, kk_ref, ka_ref,
                    wup_ref, aup_ref, gup_ref,
                    r_out, k_out, v_out, lw_out, kk_out, a_out, g_out, *, ts):
    first = pl.program_id(1) == 0
    row = lax.broadcasted_iota(jnp.int32, (ts, 1), 0)

    def token_shift(cur_ref, prev_ref, mu):
        cur = cur_ref[0]
        last = jnp.where(first, 0.0, prev_ref[0][7:8, :])
        prev = jnp.where(row == 0, last, pltpu.roll(cur, 1, 0))
        return cur + (prev - cur) * mu

    xm = token_shift(x_ref, xprev_ref, mu_ref[...])
    xl = token_shift(l_ref, lprev_ref, mul_ref[...])
    r = xm[:, :RW_WIDTH]
    k = xm[:, RW_WIDTH:2 * RW_WIDTH]
    v = xm[:, 2 * RW_WIDTH:]
    z = -(w0_ref[...] + _dot_exact(jnp.tanh(xl), wup_ref[...]))
    softplus = jnp.maximum(z, 0.0) + jnp.log(1.0 + jnp.exp(-jnp.abs(z)))
    w_log = -softplus - 0.5
    a = _sigmoid(a0_ref[...] + _dot_exact(xl, aup_ref[...]))
    g = _dot_exact(_sigmoid(xl), gup_ref[...])
    r_out[0] = r
    k_out[0] = k * (1.0 + (a - 1.0) * ka_ref[...])
    v_out[0] = v
    lw_out[0] = -jnp.exp(w_log)
    kk_out[0] = k * kk_ref[...]
    a_out[0] = a
    g_out[0] = g


def _rwkv_prep(prw, plora, mu, w0, w_up, a0, a_up, g_up, k_k, k_a, ts=256):
    b, s, w3 = prw.shape
    ts = min(ts, s)
    wl = plora.shape[-1]
    pad_rows = lambda m, lo: jnp.zeros((wl, RW_WIDTH), F32).at[lo:lo + m.shape[0]].set(m)
    wup = pad_rows(w_up, 0)
    aup = pad_rows(a_up, DECAY_LORA)
    gup = pad_rows(g_up, DECAY_LORA + AAA_LORA)
    mu_main = mu[:w3].reshape(1, w3)
    mu_lora = jnp.zeros((1, wl), F32).at[0, :RW_LORA].set(mu[w3:])
    row = lambda t: t.reshape(1, RW_WIDTH)
    blk = lambda w: pl.BlockSpec((1, ts, w), lambda bi, i: (bi, i, 0))
    prev = lambda w: pl.BlockSpec((1, 8, w), lambda bi, i: (bi, jnp.maximum(i * (ts // 8) - 1, 0), 0))
    full = lambda r, c: pl.BlockSpec((r, c), lambda bi, i: (0, 0))
    out_sds = jax.ShapeDtypeStruct((b, s, RW_WIDTH), F32)
    return pl.pallas_call(
        functools.partial(_rwkv_prep_body, ts=ts),
        grid=(b, s // ts),
        in_specs=[blk(w3), prev(w3), blk(wl), prev(wl), full(1, w3), full(1, wl),
                  full(1, RW_WIDTH), full(1, RW_WIDTH), full(1, RW_WIDTH), full(1, RW_WIDTH),
                  full(wl, RW_WIDTH), full(wl, RW_WIDTH), full(wl, RW_WIDTH)],
        out_specs=[blk(RW_WIDTH)] * 7,
        out_shape=[out_sds] * 7,
        compiler_params=_params("parallel", "parallel"),
        name="rwkv_prep",
    )(prw, prw, plora, plora, mu_main, mu_lora, row(w0), row(a0), row(k_k), row(k_a), wup, aup, gup)


def _rwkv_chunk_body(r_ref, k_ref, v_ref, lw_ref, kk_ref, a_ref, g_ref, rk_ref, lnw_ref, lnb_ref,
                     o_ref, st_ref, *, chunk, heads):
    @pl.when(pl.program_id(1) == 0)
    def _():
        st_ref[...] = jnp.zeros_like(st_ref)

    row = lax.broadcasted_iota(jnp.int32, (chunk, chunk), 0)
    col = lax.broadcasted_iota(jnp.int32, (chunk, chunk), 1)
    lower = col <= row
    strict = col < row
    ones_lower = jnp.where(lower, 1.0, 0.0).astype(F32)
    eye = jnp.where(row == col, 1.0, 0.0).astype(F32)
    n_doublings = int(math.log2(chunk)) - 1

    def head(h, carry):
        r = r_ref[0, h]
        k = k_ref[0, h]
        v = v_ref[0, h]
        lw = lw_ref[0, h]
        kkr = kk_ref[0, h]
        a_gate = a_ref[0, h]
        kk = kkr / jnp.maximum(jnp.sqrt(jnp.sum(kkr * kkr, axis=-1, keepdims=True)), 1e-12)
        cw = _dot_exact(ones_lower, lw)
        w_in = jnp.exp(cw)
        w_inv = jnp.exp(-cw)
        a_t = -kk * jnp.exp(cw - lw)
        r_t = r * w_in
        b_t = kk * a_gate * w_inv
        k_t = k * w_inv
        lhs = jnp.concatenate([a_t, r_t], axis=0)
        gb = _dot_exact(lhs, b_t, _NT)
        gk = _dot_exact(lhs, k_t, _NT)
        n_ab = jnp.where(strict, gb[:chunk], 0.0)
        a_ak = jnp.where(strict, gk[:chunk], 0.0)
        m_rb = jnp.where(lower, gb[chunk:], 0.0)
        m_rk = jnp.where(lower, gk[chunk:], 0.0)
        inv = eye + n_ab
        power = n_ab
        for _ in range(n_doublings):
            power = _dot_exact(power, power)
            inv = inv + _dot_exact(inv, power)
        st = st_ref[h]
        from_state = _dot_exact(lhs, st, _NT)
        u = _dot_exact(inv, from_state[:chunk] + _dot_exact(a_ak, v))
        y = from_state[chunk:] + _dot_exact(m_rb, u) + _dot_exact(m_rk, v)
        w_end = cw[chunk - 1:chunk]
        tail = jnp.exp(w_end - cw)
        st_ref[h] = (st * jnp.exp(w_end) + _dot_exact(u, kk * a_gate * tail, _TN)
                     + _dot_exact(v, k * tail, _TN))
        mean = jnp.mean(y, axis=-1, keepdims=True)
        var = jnp.mean(jnp.square(y - mean), axis=-1, keepdims=True)
        yn = (y - mean) * lax.rsqrt(var + RW_GN_EPS) * lnw_ref[h] + lnb_ref[h]
        bonus = jnp.sum(r * k * rk_ref[h], axis=-1, keepdims=True) * v
        o_ref[0, h] = ((yn + bonus) * g_ref[0, h]).astype(o_ref.dtype)
        return carry

    lax.fori_loop(0, heads, head, 0)


def _rwkv_chunked(r, k, v, lw, kk, a, g, r_k, ln_w, ln_b, chunk=RW_CHUNK):
    b, h, s, d = r.shape
    chunk = min(chunk, s)
    blk = pl.BlockSpec((1, h, chunk, d), lambda bi, c: (bi, 0, c, 0))
    par = pl.BlockSpec((h, 1, d), lambda bi, c: (0, 0, 0))
    per_head = lambda t: t.reshape(h, 1, d)
    return pl.pallas_call(
        functools.partial(_rwkv_chunk_body, chunk=chunk, heads=h),
        grid=(b, s // chunk),
        in_specs=[blk] * 7 + [par] * 3,
        out_specs=blk,
        out_shape=jax.ShapeDtypeStruct((b, h, s, d), F32),
        scratch_shapes=[pltpu.VMEM((h, d, d), F32)],
        compiler_params=_params("parallel", "arbitrary"),
        name="rwkv_chunk",
    )(r, k, v, lw, kk, a, g, per_head(r_k), per_head(ln_w), per_head(ln_b))


def _hgrn_body(lbraw_ref, ng_ref, q_ref, f_ref, i_ref, g_ref, o_ref, st_ref, *, block, layer):
    @pl.when(pl.program_id(2) == 0)
    def _():
        st_ref[...] = jnp.zeros_like(st_ref)

    lbraw = lbraw_ref[...]
    e = jnp.exp(lbraw - jnp.max(lbraw, axis=0, keepdims=True))
    soft = e / jnp.sum(e, axis=0, keepdims=True)
    lb = jnp.sum(soft[:layer + 1], axis=0, keepdims=True) - soft[0:1]
    ng = ng_ref[...]

    c, sub = HG_CHUNK, HG_SUB
    n_sub = c // sub
    row = lax.broadcasted_iota(jnp.int32, (c, c), 0)
    col = lax.broadcasted_iota(jnp.int32, (c, c), 1)
    ones_lower = jnp.where(col <= row, 1.0, 0.0).astype(F32)
    ones_sq = jnp.ones((LANES, LANES), BF16)
    ri = lax.broadcasted_iota(jnp.int32, (c, 1), 0)
    rs = lax.broadcasted_iota(jnp.int32, (sub, 1), 0)
    zeros_sub = jnp.zeros((sub, LANES), F32)

    def chunk_step(ci, carry):
        off = pl.multiple_of(ci * c, c)
        q = q_ref[pl.ds(off, c), :]
        iv = i_ref[pl.ds(off, c), :]
        gg = g_ref[pl.ds(off, c), :]
        fg = lb + (1.0 - lb) * _sigmoid(f_ref[pl.ds(off, c), :])
        k = 1.0 - fg
        b = _dot_exact(ones_lower, jnp.log(fg))
        b_end = b[c - 1:c]
        st = st_ref[...]
        o = _dot_bf16(q * jnp.exp(b), st, _NT)

        brefs = [b[sub * i - 1:sub * i] for i in range(1, n_sub)]
        bref_rows = jnp.concatenate([zeros_sub] + [jnp.broadcast_to(t, (sub, LANES)) for t in brefs], axis=0)
        qh = q * jnp.exp(jnp.where(ri >= sub, b - bref_rows, NEG_BIG))
        off_parts = [zeros_sub]
        for i in range(1, n_sub):
            kh = k * jnp.exp(jnp.where(ri < sub * i, brefs[i - 1] - b, NEG_BIG))
            a_i = _dot_bf16(qh[sub * i:sub * (i + 1)], kh, _NT)
            off_parts.append(_dot_bf16(a_i, iv))
        o = o + jnp.concatenate(off_parts, axis=0)

        diag_parts = []
        for i in range(n_sub):
            sl = slice(sub * i, sub * (i + 1))
            qi, ki, bi, ii = q[sl], k[sl], b[sl], iv[sl]
            pieces = []
            for s in range(sub):
                decay = jnp.exp(jnp.where(rs >= s, bi - bi[s:s + 1], NEG_BIG))
                pieces.append(qi * decay * ki[s:s + 1])
            stacked = jnp.concatenate(pieces, axis=0)
            hi = stacked.astype(BF16)
            lo = (stacked - hi.astype(F32)).astype(BF16)
            sums = (jnp.dot(hi, ones_sq, preferred_element_type=F32)
                    + jnp.dot(lo, ones_sq, preferred_element_type=F32))
            od = zeros_sub
            for s in range(sub):
                od = od + sums[sub * s:sub * (s + 1)] * ii[s:s + 1]
            diag_parts.append(od)
        o = o + jnp.concatenate(diag_parts, axis=0)

        o = o * lax.rsqrt(jnp.mean(o * o, axis=-1, keepdims=True) + EPS) * ng
        o_ref[pl.ds(off, c), :] = (o * (gg * _sigmoid(gg))).astype(o_ref.dtype)
        st_ref[...] = st * jnp.exp(b_end) + _dot_bf16(iv, k * jnp.exp(b_end - b), _TN)
        return carry

    lax.fori_loop(0, block // c, chunk_step, 0)


def _hgrn2(p, lb_raw, norm_g, batch, seq, layer, block=512):
    block = min(block, seq)
    nb = seq // block
    depth = lb_raw.shape[0]
    col = lambda part: pl.BlockSpec((block, LANES), lambda b, h, i: (b * nb + i, part * HG_HEADS + h))
    return pl.pallas_call(
        functools.partial(_hgrn_body, block=block, layer=layer),
        grid=(batch, HG_HEADS, nb),
        in_specs=[pl.BlockSpec((depth, LANES), lambda b, h, i: (0, h)),
                  pl.BlockSpec((1, LANES), lambda b, h, i: (0, 0)),
                  col(0), col(1), col(2), col(3)],
        out_specs=pl.BlockSpec((block, LANES), lambda b, h, i: (b * nb + i, h)),
        out_shape=jax.ShapeDtypeStruct((batch * seq, D_MODEL), BF16),
        scratch_shapes=[pltpu.VMEM((LANES, LANES), F32)],
        compiler_params=_params("parallel", "parallel", "arbitrary"),
        name="hgrn2",
    )(lb_raw, norm_g.reshape(1, LANES), p, p, p, p)


def _ffn(x, norm_g, w_gate, w_up, w_down):
    h = _rmsnorm(x, norm_g, BF16)
    hidden = _ffn_up(h, w_gate.astype(BF16), w_up.astype(BF16))
    return _matmul_residual(hidden, w_down.astype(BF16), x)


def _even_layer(x, batch, seq, layer, norm_g, w_in, w_out, lam_vec, subln_g, mu, w0, w_up, a0, a_up, g_up,
                k_k, k_a, r_k, ln_w, ln_b):
    h = _rmsnorm(x, norm_g, BF16)
    w_in = w_in.astype(BF16)
    qk = _qk_proj(h, w_in[:, :2 * DA_WIDTH], seq)
    v = _matmul(h, w_in[:, 2 * DA_WIDTH:3 * DA_WIDTH], BF16)
    lam_init = 0.8 - 0.6 * math.exp(-0.3 * layer)
    a_out = _diff_attention(qk, v, lam_vec, subln_g, batch, seq, lam_init)

    rw0 = 3 * DA_WIDTH
    prw = _matmul(h, w_in[:, rw0:rw0 + 3 * RW_WIDTH], F32)
    w_lora = jnp.pad(w_in[:, rw0 + 3 * RW_WIDTH:], ((0, 0), (0, RW_LORA_PAD - RW_LORA)))
    plora = _matmul(h, w_lora, F32)
    tok = _rwkv_prep(prw.reshape(batch, seq, -1), plora.reshape(batch, seq, -1), mu, w0, w_up, a0, a_up, g_up,
                     k_k, k_a)
    to_heads = lambda t: t.reshape(batch, seq, RW_HEADS, RW_HEAD).transpose(0, 2, 1, 3)
    y = _rwkv_chunked(*[to_heads(t) for t in tok], r_k, ln_w.reshape(RW_HEADS, RW_HEAD),
                      ln_b.reshape(RW_HEADS, RW_HEAD))
    b_out = y.transpose(0, 2, 1, 3).reshape(batch * seq, RW_WIDTH).astype(BF16)

    w_out = w_out.astype(BF16)
    return _matmul2_residual(a_out, b_out, w_out[:DA_WIDTH], w_out[DA_WIDTH:], x)


def _odd_layer(x, batch, seq, layer, norm_g, w_in, w_out, lb_raw, hg_norm_g):
    h = _rmsnorm(x, norm_g, BF16)
    p = _matmul(h, w_in.astype(BF16), F32)
    o = _hgrn2(p, lb_raw, hg_norm_g, batch, seq, layer)
    return _matmul_residual(o, w_out.astype(BF16), x, tm=1024)


def kernel(x, attn_norm_g, ffn_norm_g, final_norm_g, even_w_in, even_w_out, da_lambda, da_subln_g, rw_mu, rw_w0, rw_w_up, rw_a0, rw_a_up, rw_g_up, rw_k_k, rw_k_a, rw_r_k, rw_ln_w, rw_ln_b, odd_w_in, odd_w_out, hg_lower_bound, hg_norm_g, ffn_w_gate, ffn_w_up, ffn_w_down):
    batch, seq, d = x.shape
    depth = attn_norm_g.shape[0]
    xs = x.reshape(batch * seq, d)
    for layer in range(depth):
        j = layer // 2
        if layer % 2 == 0:
            xs = _even_layer(xs, batch, seq, layer, attn_norm_g[layer], even_w_in[j], even_w_out[j], da_lambda[j],
                             da_subln_g[j], rw_mu[j], rw_w0[j], rw_w_up[j], rw_a0[j], rw_a_up[j], rw_g_up[j],
                             rw_k_k[j], rw_k_a[j], rw_r_k[j], rw_ln_w[j], rw_ln_b[j])
        else:
            xs = _odd_layer(xs, batch, seq, layer, attn_norm_g[layer], odd_w_in[j], odd_w_out[j],
                            hg_lower_bound, hg_norm_g[j])
        xs = _ffn(xs, ffn_norm_g[layer], ffn_w_gate[layer], ffn_w_up[layer], ffn_w_down[layer])
    return _rmsnorm(xs, final_norm_g, F32).reshape(batch, seq, d)
```

```python
import functools
import math

import jax
import jax.numpy as jnp
from jax import lax
from jax.experimental import pallas as pl
from jax.experimental.pallas import tpu as pltpu

D_MODEL = 2048
EPS = 1e-6

DA_WIDTH = D_MODEL // 2
DA_HEADS = 8
DA_HEAD_DIM = DA_WIDTH // DA_HEADS // 2
DA_V_DIM = 2 * DA_HEAD_DIM
ROPE_DIM = DA_HEAD_DIM // 4
ROPE_THETA = 500000.0

RW_WIDTH = D_MODEL // 2
RW_HEAD = 64
RW_HEADS = RW_WIDTH // RW_HEAD
DECAY_LORA = 64
AAA_LORA = 64
GATE_LORA = 32
RW_LORA = DECAY_LORA + AAA_LORA + GATE_LORA
RW_LORA_PAD = 256
RW_GN_EPS = 64e-5
RW_CHUNK = 64

HG_EXPAND = 128
HG_HEADS = D_MODEL // HG_EXPAND
HG_CHUNK = 64
HG_SUB = 8

LANES = 128
NEG_BIG = -1e30
LOG2_E = 1.0 / math.log(2.0)
VMEM_LIMIT = 48 * 1024 * 1024

F32 = jnp.float32
BF16 = jnp.bfloat16
HIGHEST = lax.Precision.HIGHEST

_NN = (((1,), (0,)), ((), ()))
_NT = (((1,), (1,)), ((), ()))
_TN = (((0,), (0,)), ((), ()))


def _dot_exact(a, b, dims=_NN):
    return lax.dot_general(a, b, dims, precision=HIGHEST, preferred_element_type=F32)


def _dot_bf16(a, b, dims=_NN):
    return lax.dot_general(a.astype(BF16), b.astype(BF16), dims, preferred_element_type=F32)


def _sigmoid(x):
    return 1.0 / (1.0 + jnp.exp(-x))


def _params(*sem):
    return pltpu.CompilerParams(dimension_semantics=sem, vmem_limit_bytes=VMEM_LIMIT)


def _rmsnorm_body(x_ref, g_ref, o_ref):
    x = x_ref[...]
    ms = jnp.mean(x * x, axis=-1, keepdims=True)
    o_ref[...] = (x * lax.rsqrt(ms + EPS) * g_ref[...]).astype(o_ref.dtype)


def _rmsnorm(x2d, g, out_dtype, tm=512):
    t, d = x2d.shape
    tm = min(tm, t)
    return pl.pallas_call(
        _rmsnorm_body,
        grid=(t // tm,),
        in_specs=[pl.BlockSpec((tm, d), lambda i: (i, 0)), pl.BlockSpec((1, d), lambda i: (0, 0))],
        out_specs=pl.BlockSpec((tm, d), lambda i: (i, 0)),
        out_shape=jax.ShapeDtypeStruct((t, d), out_dtype),
        compiler_params=_params("parallel"),
        name="rmsnorm",
    )(x2d, g.reshape(1, d))


def _mm_body(a_ref, w_ref, o_ref):
    o_ref[...] = jnp.dot(a_ref[...], w_ref[...], preferred_element_type=F32).astype(o_ref.dtype)


def _matmul(a, w, out_dtype, tm=1024, tn=512):
    m, k = a.shape
    n = w.shape[1]
    tm, tn = min(tm, m), min(tn, n)
    return pl.pallas_call(
        _mm_body,
        grid=(m // tm, n // tn),
        in_specs=[pl.BlockSpec((tm, k), lambda i, j: (i, 0)), pl.BlockSpec((k, tn), lambda i, j: (0, j))],
        out_specs=pl.BlockSpec((tm, tn), lambda i, j: (i, j)),
        out_shape=jax.ShapeDtypeStruct((m, n), out_dtype),
        compiler_params=_params("parallel", "arbitrary"),
        name="matmul",
    )(a, w)


def _mm_res_body(a_ref, w_ref, r_ref, o_ref):
    o_ref[...] = r_ref[...] + jnp.dot(a_ref[...], w_ref[...], preferred_element_type=F32)


def _matmul_residual(a, w, res, tm=512, tn=512):
    m, k = a.shape
    n = w.shape[1]
    tm, tn = min(tm, m), min(tn, n)
    return pl.pallas_call(
        _mm_res_body,
        grid=(m // tm, n // tn),
        in_specs=[
            pl.BlockSpec((tm, k), lambda i, j: (i, 0)),
            pl.BlockSpec((k, tn), lambda i, j: (0, j)),
            pl.BlockSpec((tm, tn), lambda i, j: (i, j)),
        ],
        out_specs=pl.BlockSpec((tm, tn), lambda i, j: (i, j)),
        out_shape=jax.ShapeDtypeStruct((m, n), F32),
        compiler_params=_params("parallel", "arbitrary"),
        name="matmul_residual",
    )(a, w, res)


def _mm2_res_body(a1_ref, a2_ref, w1_ref, w2_ref, r_ref, o_ref):
    acc = jnp.dot(a1_ref[...], w1_ref[...], preferred_element_type=F32)
    acc = acc + jnp.dot(a2_ref[...], w2_ref[...], preferred_element_type=F32)
    o_ref[...] = r_ref[...] + acc


def _matmul2_residual(a1, a2, w1, w2, res, tm=1024, tn=512):
    m, k1 = a1.shape
    k2 = a2.shape[1]
    n = w1.shape[1]
    tm, tn = min(tm, m), min(tn, n)
    return pl.pallas_call(
        _mm2_res_body,
        grid=(m // tm, n // tn),
        in_specs=[
            pl.BlockSpec((tm, k1), lambda i, j: (i, 0)),
            pl.BlockSpec((tm, k2), lambda i, j: (i, 0)),
            pl.BlockSpec((k1, tn), lambda i, j: (0, j)),
            pl.BlockSpec((k2, tn), lambda i, j: (0, j)),
            pl.BlockSpec((tm, tn), lambda i, j: (i, j)),
        ],
        out_specs=pl.BlockSpec((tm, tn), lambda i, j: (i, j)),
        out_shape=jax.ShapeDtypeStruct((m, n), F32),
        compiler_params=_params("parallel", "arbitrary"),
        name="out_proj_residual",
    )(a1, a2, w1, w2, res)


def _ffn_up_body(h_ref, wg_ref, wu_ref, o_ref):
    h = h_ref[...]
    g = jnp.dot(h, wg_ref[...], preferred_element_type=F32)
    u = jnp.dot(h, wu_ref[...], preferred_element_type=F32)
    o_ref[...] = (g * _sigmoid(g) * u).astype(o_ref.dtype)


def _ffn_up(h, wg, wu, tm=1024, tn=512):
    m, k = h.shape
    n = wg.shape[1]
    tm, tn = min(tm, m), min(tn, n)
    return pl.pallas_call(
        _ffn_up_body,
        grid=(m // tm, n // tn),
        in_specs=[
            pl.BlockSpec((tm, k), lambda i, j: (i, 0)),
            pl.BlockSpec((k, tn), lambda i, j: (0, j)),
            pl.BlockSpec((k, tn), lambda i, j: (0, j)),
        ],
        out_specs=pl.BlockSpec((tm, tn), lambda i, j: (i, j)),
        out_shape=jax.ShapeDtypeStruct((m, n), BF16),
        compiler_params=_params("parallel", "arbitrary"),
        name="ffn_up",
    )(h, wg, wu)


def _qk_rope_body(h_ref, w_ref, cos_ref, sin_lo_ref, sin_hi_ref, o_ref, *, n_q_tiles, tn, q_scale):
    acc = jnp.dot(h_ref[...], w_ref[...], preferred_element_type=F32)
    scale = jnp.where(pl.program_id(1) < n_q_tiles, q_scale, 1.0).astype(F32)
    cos = cos_ref[...]
    sin_lo = sin_lo_ref[...]
    sin_hi = sin_hi_ref[...]
    half = ROPE_DIM // 2
    for g in range(tn // LANES):
        x = acc[:, g * LANES:(g + 1) * LANES]
        y = x * cos + pltpu.roll(x, half, 1) * sin_hi + pltpu.roll(x, LANES - half, 1) * sin_lo
        o_ref[:, g * LANES:(g + 1) * LANES] = (y * scale).astype(o_ref.dtype)


def _rope_tables(seq):
    pos = jnp.arange(seq, dtype=F32)
    inv_freq = 1.0 / (ROPE_THETA ** (jnp.arange(0, ROPE_DIM, 2, dtype=F32) / ROPE_DIM))
    ang = pos[:, None] * inv_freq[None, :]
    cos, sin = jnp.cos(ang), jnp.sin(ang)
    half = ROPE_DIM // 2
    pad = jnp.zeros((seq, DA_HEAD_DIM - ROPE_DIM), F32)
    zer = jnp.zeros((seq, half), F32)
    cos_c = jnp.concatenate([cos, cos, pad + 1.0], axis=1)
    sin_lo = jnp.concatenate([-sin, zer, pad], axis=1)
    sin_hi = jnp.concatenate([zer, sin, pad], axis=1)
    rep = lambda t: jnp.concatenate([t, t], axis=1)
    return rep(cos_c), rep(sin_lo), rep(sin_hi)


def _qk_proj(h, w_qk, seq, tm=1024, tn=512):
    m, k = h.shape
    n = w_qk.shape[1]
    tm, tn = min(tm, seq), min(tn, n)
    cos, sin_lo, sin_hi = _rope_tables(seq)
    s_tiles = seq // tm
    body = functools.partial(_qk_rope_body, n_q_tiles=DA_WIDTH // tn, tn=tn, q_scale=DA_HEAD_DIM ** -0.5 * LOG2_E)
    tab = pl.BlockSpec((tm, LANES), lambda i, j: (i % s_tiles, 0))
    return pl.pallas_call(
        body,
        grid=(m // tm, n // tn),
        in_specs=[pl.BlockSpec((tm, k), lambda i, j: (i, 0)), pl.BlockSpec((k, tn), lambda i, j: (0, j)),
                  tab, tab, tab],
        out_specs=pl.BlockSpec((tm, tn), lambda i, j: (i, j)),
        out_shape=jax.ShapeDtypeStruct((m, n), BF16),
        compiler_params=_params("parallel", "arbitrary"),
        name="qk_proj_rope",
    )(h, w_qk, cos, sin_lo, sin_hi)


def _attn_body(lam_ref, g_ref, q_ref, k_ref, v_ref, o_ref, *, tq, lam_init, group):
    qi = pl.program_id(2)
    lane = lax.broadcasted_iota(jnp.int32, (1, LANES), 1)
    lanes = [slice(h * LANES, (h + 1) * LANES) for h in range(group)]

    def split_components(q):
        zero = jnp.zeros_like(q)
        return jnp.concatenate([jnp.where(lane < DA_HEAD_DIM, q, zero), jnp.where(lane >= DA_HEAD_DIM, q, zero)],
                               axis=0)
    qs = [split_components(q_ref[:, ln]) for ln in lanes]

    def step(j, carry, masked):
        rows = pl.ds(pl.multiple_of(j * tq, tq), tq)
        s = [lax.dot_general(k_ref[rows, ln], q, _NT, preferred_element_type=F32)
             for ln, q in zip(lanes, qs)]
        if masked:
            key = lax.broadcasted_iota(jnp.int32, (tq, 2 * tq), 0)
            qry = lax.broadcasted_iota(jnp.int32, (tq, 2 * tq), 1)
            keep = key <= jnp.where(qry >= tq, qry - tq, qry)
            s = [jnp.where(keep, t, NEG_BIG) for t in s]
        m_new = [jnp.maximum(c[0], jnp.max(t, axis=0, keepdims=True)) for c, t in zip(carry, s)]
        p = [jnp.exp2(t - m) for t, m in zip(s, m_new)]
        pv = [lax.dot_general(v_ref[rows, ln], t.astype(BF16), _TN, preferred_element_type=F32)
              for ln, t in zip(lanes, p)]
        out = []
        for (m, l, acc), mn, t, u in zip(carry, m_new, p, pv):
            alpha = jnp.exp2(m - mn)
            out.append((mn, alpha * l + jnp.sum(t, axis=0, keepdims=True), alpha * acc + u))
        return tuple(out)

    init = tuple((jnp.full((1, 2 * tq), NEG_BIG, F32), jnp.zeros((1, 2 * tq), F32),
                  jnp.zeros((LANES, 2 * tq), F32)) for _ in range(group))
    carry = lax.fori_loop(0, qi, lambda j, c: step(j, c, False), init)
    carry = step(qi, carry, True)

    lf = lam_ref[...]
    lam = (jnp.exp(jnp.sum(lf[0:1] * lf[1:2], axis=-1, keepdims=True))
           - jnp.exp(jnp.sum(lf[2:3] * lf[3:4], axis=-1, keepdims=True)) + lam_init)
    for ln, (_, l, acc) in zip(lanes, carry):
        o = acc[:, :tq] / l[:, :tq] - lam * (acc[:, tq:] / l[:, tq:])
        o = o * lax.rsqrt(jnp.mean(o * o, axis=0, keepdims=True) + EPS)
        o_ref[:, ln] = (o.T * g_ref[...] * (1.0 - lam_init)).astype(o_ref.dtype)


def _diff_attention(qk, v, lam_vec, subln_g, batch, seq, lam_init, tq=256, group=2):
    tq = min(tq, seq)
    nq = seq // tq
    n_groups = DA_HEADS // group
    width = group * LANES
    body = functools.partial(_attn_body, tq=tq, lam_init=lam_init, group=group)
    return pl.pallas_call(
        body,
        grid=(batch, n_groups, nq),
        in_specs=[
            pl.BlockSpec((4, DA_HEAD_DIM), lambda b, h, i: (0, 0)),
            pl.BlockSpec((1, DA_V_DIM), lambda b, h, i: (0, 0)),
            pl.BlockSpec((tq, width), lambda b, h, i: (b * nq + i, h)),
            pl.BlockSpec((seq, width), lambda b, h, i: (b, n_groups + h)),
            pl.BlockSpec((seq, width), lambda b, h, i: (b, h)),
        ],
        out_specs=pl.BlockSpec((tq, width), lambda b, h, i: (b * nq + i, h)),
        out_shape=jax.ShapeDtypeStruct((batch * seq, DA_WIDTH), BF16),
        compiler_params=_params("parallel", "parallel", "arbitrary"),
        name="diff_attention",
    )(lam_vec, subln_g.reshape(1, DA_V_DIM), qk, qk, v)


def _rwkv_prep_body(x_ref, xprev_ref, l_ref, lprev_ref, mu_ref, mul_ref, w0_ref, a0_ref, kk_ref, ka_ref,
                    wup_ref, aup_ref, gup_ref,
                    r_out, k_out, v_out, lw_out, kk_out, a_out, g_out, *, ts):
    first = pl.program_id(1) == 0
    row = lax.broadcasted_iota(jnp.int32, (ts, 1), 0)

    def token_shift(cur_ref, prev_ref, mu):
        cur = cur_ref[0]
        last = jnp.where(first, 0.0, prev_ref[0][7:8, :])
        prev = jnp.where(row == 0, last, pltpu.roll(cur, 1, 0))
        return cur + (prev - cur) * mu

    xm = token_shift(x_ref, xprev_ref, mu_ref[...])
    xl = token_shift(l_ref, lprev_ref, mul_ref[...])
    r = xm[:, :RW_WIDTH]
    k = xm[:, RW_WIDTH:2 * RW_WIDTH]
    v = xm[:, 2 * RW_WIDTH:]
    z = -(w0_ref[...] + _dot_exact(jnp.tanh(xl), wup_ref[...]))
    softplus = jnp.maximum(z, 0.0) + jnp.log(1.0 + jnp.exp(-jnp.abs(z)))
    w_log = -softplus - 0.5
    a = _sigmoid(a0_ref[...] + _dot_exact(xl, aup_ref[...]))
    g = _dot_exact(_sigmoid(xl), gup_ref[...])
    r_out[0] = r
    k_out[0] = k * (1.0 + (a - 1.0) * ka_ref[...])
    v_out[0] = v
    lw_out[0] = -jnp.exp(w_log)
    kk_out[0] = k * kk_ref[...]
    a_out[0] = a
    g_out[0] = g


def _rwkv_prep(prw, plora, mu, w0, w_up, a0, a_up, g_up, k_k, k_a, ts=256):
    b, s, w3 = prw.shape
    ts = min(ts, s)
    wl = plora.shape[-1]
    pad_rows = lambda m, lo: jnp.zeros((wl, RW_WIDTH), F32).at[lo:lo + m.shape[0]].set(m)
    wup = pad_rows(w_up, 0)
    aup = pad_rows(a_up, DECAY_LORA)
    gup = pad_rows(g_up, DECAY_LORA + AAA_LORA)
    mu_main = mu[:w3].reshape(1, w3)
    mu_lora = jnp.zeros((1, wl), F32).at[0, :RW_LORA].set(mu[w3:])
    row = lambda t: t.reshape(1, RW_WIDTH)
    blk = lambda w: pl.BlockSpec((1, ts, w), lambda bi, i: (bi, i, 0))
    prev = lambda w: pl.BlockSpec((1, 8, w), lambda bi, i: (bi, jnp.maximum(i * (ts // 8) - 1, 0), 0))
    full = lambda r, c: pl.BlockSpec((r, c), lambda bi, i: (0, 0))
    out_sds = jax.ShapeDtypeStruct((b, s, RW_WIDTH), F32)
    return pl.pallas_call(
        functools.partial(_rwkv_prep_body, ts=ts),
        grid=(b, s // ts),
        in_specs=[blk(w3), prev(w3), blk(wl), prev(wl), full(1, w3), full(1, wl),
                  full(1, RW_WIDTH), full(1, RW_WIDTH), full(1, RW_WIDTH), full(1, RW_WIDTH),
                  full(wl, RW_WIDTH), full(wl, RW_WIDTH), full(wl, RW_WIDTH)],
        out_specs=[blk(RW_WIDTH)] * 7,
        out_shape=[out_sds] * 7,
        compiler_params=_params("parallel", "parallel"),
        name="rwkv_prep",
    )(prw, prw, plora, plora, mu_main, mu_lora, row(w0), row(a0), row(k_k), row(k_a), wup, aup, gup)


def _cumsum_rows(ones_lower_bf16, x):
    hi = x.astype(BF16)
    rem = x - hi.astype(F32)
    mid = rem.astype(BF16)
    lo = (rem - mid.astype(F32)).astype(BF16)
    width = x.shape[1]
    out = jnp.dot(ones_lower_bf16, jnp.concatenate([hi, mid, lo], axis=1), preferred_element_type=F32)
    return out[:, :width] + out[:, width:2 * width] + out[:, 2 * width:]


def _rwkv_chunk_body(r_ref, k_ref, v_ref, lw_ref, kk_ref, a_ref, g_ref, rk_ref, lnw_ref, lnb_ref,
                     o_ref, st_ref, *, chunk, n_chunks, n_pairs, group):
    @pl.when(pl.program_id(1) == 0)
    def _():
        st_ref[...] = jnp.zeros_like(st_ref)

    c = chunk
    head = RW_HEAD
    lane = lax.broadcasted_iota(jnp.int32, (1, LANES), 1)
    head0 = lane < head
    row_c = lax.broadcasted_iota(jnp.int32, (c, LANES), 0)
    col_c = lax.broadcasted_iota(jnp.int32, (c, LANES), 1)
    col_in_head = jnp.where(col_c < head, col_c, col_c - head)
    strict = col_in_head < row_c
    lower = col_in_head <= row_c
    row_p = lax.broadcasted_iota(jnp.int32, (LANES, LANES), 0)
    col_p = lax.broadcasted_iota(jnp.int32, (LANES, LANES), 1)
    eye = jnp.where(row_p == col_p, 1.0, 0.0).astype(F32)
    same_head = (row_p < head) == (col_p < head)
    ones_lower = jnp.where(lax.broadcasted_iota(jnp.int32, (c, c), 1) <= lax.broadcasted_iota(jnp.int32, (c, c), 0),
                           1.0, 0.0).astype(BF16)
    zeros_c = jnp.zeros((c, LANES), BF16)
    n_doublings = int(math.log2(c)) - 1

    def only0(x):
        return jnp.where(head0, x, jnp.zeros_like(x))

    def only1(x):
        return jnp.where(head0, jnp.zeros_like(x), x)

    def head_sum(x):
        s0 = jnp.sum(only0(x), axis=-1, keepdims=True)
        s1 = jnp.sum(only1(x), axis=-1, keepdims=True)
        return jnp.where(head0, s0, s1)

    def each(fn, *lists):
        return [fn(*args) for args in zip(*lists)]

    def chunk_group(rows, pairs):
        lanes = [slice(p * LANES, (p + 1) * LANES) for p in pairs]
        load = lambda ref: [ref[0, rows, ln] for ln in lanes]
        r, k, v, lw, kkr, ag = load(r_ref), load(k_ref), load(v_ref), load(lw_ref), load(kk_ref), load(a_ref)
        kk = each(lambda t: t / jnp.maximum(jnp.sqrt(head_sum(t * t)), 1e-12), kkr)
        bm = each(lambda a, b: a * b, kk, ag)
        cw = each(lambda t: _cumsum_rows(ones_lower, t), lw)
        w_inv = each(lambda t: jnp.exp(-t), cw)
        b_t = each(lambda a, b: (a * b).astype(BF16), bm, w_inv)
        k_t = each(lambda a, b: (a * b).astype(BF16), k, w_inv)
        lhs = each(lambda kk_, cw_, lw_, r_: jnp.concatenate([-kk_ * jnp.exp(cw_ - lw_), r_ * jnp.exp(cw_)],
                                                             axis=0).astype(BF16), kk, cw, lw, r)
        g0 = each(lambda l, b, k_: lax.dot_general(only0(l), jnp.concatenate([b, k_], axis=0), _NT,
                                                   preferred_element_type=F32), lhs, b_t, k_t)
        g1 = each(lambda l, b, k_: lax.dot_general(only1(l), jnp.concatenate([k_, b], axis=0), _NT,
                                                   preferred_element_type=F32), lhs, b_t, k_t)
        top0 = each(lambda g_: jnp.where(strict, g_[:c], 0.0), g0)
        top1 = each(lambda g_: jnp.where(strict, g_[:c], 0.0), g1)
        bot = each(lambda a, b: jnp.concatenate([jnp.where(lower, a[c:], 0.0), jnp.where(lower, b[c:], 0.0)],
                                                axis=1).astype(BF16), g0, g1)
        power = each(lambda a, b: jnp.concatenate([only0(a), only1(b)], axis=0), top0, top1)
        inv = each(lambda n: eye + n, power)
        for _ in range(n_doublings):
            power = each(lambda t: jnp.dot(t.astype(BF16), t.astype(BF16), preferred_element_type=F32), power)
            inv = each(lambda t, pw: t + jnp.dot(t.astype(BF16), pw.astype(BF16), preferred_element_type=F32),
                       inv, power)
        st = [st_ref[p] for p in pairs]
        from_state = each(lambda l, s_: lax.dot_general(l, s_.astype(BF16), _NT, preferred_element_type=F32),
                          lhs, st)
        vb = each(lambda t: t.astype(BF16), v)
        x = each(lambda fs, a, b, v_: fs[:c] + jnp.dot(
            jnp.concatenate([a, b], axis=1).astype(BF16),
            jnp.concatenate([zeros_c, only0(v_), only1(v_), zeros_c], axis=0), preferred_element_type=F32),
            from_state, top0, top1, vb)
        tu = each(lambda t, x_: jnp.dot(t.astype(BF16), jnp.concatenate([only0(x_), only1(x_)], axis=0).astype(BF16),
                                        preferred_element_type=F32), inv, x)
        ub = each(lambda t: (t[:c] + t[c:]).astype(BF16), tu)
        y = each(lambda fs, b, u_, v_: fs[c:] + jnp.dot(
            b, jnp.concatenate([only0(u_), only0(v_), only1(v_), only1(u_)], axis=0), preferred_element_type=F32),
            from_state, bot, ub, vb)
        w_end = each(lambda t: t[c - 1:c], cw)
        tail = each(lambda e, t: jnp.exp(e - t), w_end, cw)
        upd = each(lambda u_, v_, bm_, k_, tl: lax.dot_general(
            jnp.concatenate([u_, v_], axis=0),
            jnp.concatenate([(bm_ * tl).astype(BF16), (k_ * tl).astype(BF16)], axis=0), _TN,
            preferred_element_type=F32), ub, vb, bm, k, tail)
        for p, s_, e, up in zip(pairs, st, w_end, upd):
            st_ref[p] = s_ * jnp.exp(e) + jnp.where(same_head, up, 0.0)
        for ln, y_, r_, k_, v_ in zip(lanes, y, r, k, v):
            yc = y_ - head_sum(y_) * (1.0 / head)
            var = head_sum(yc * yc) * (1.0 / head)
            yn = yc * lax.rsqrt(var + RW_GN_EPS) * lnw_ref[:, ln] + lnb_ref[:, ln]
            bonus = head_sum(r_ * k_ * rk_ref[:, ln]) * v_
            o_ref[rows, ln] = ((yn + bonus) * g_ref[0, rows, ln]).astype(o_ref.dtype)

    for ci in range(n_chunks):
        for p0 in range(0, n_pairs, group):
            chunk_group(slice(ci * c, (ci + 1) * c), list(range(p0, min(p0 + group, n_pairs))))


def _rwkv_chunked(r, k, v, lw, kk, a, g, r_k, ln_w, ln_b, chunk=RW_CHUNK, block=RW_CHUNK, group=8):
    b, s, w = r.shape
    chunk = min(chunk, s)
    block = min(block, s)
    nb = s // block
    n_pairs = w // LANES
    blk = pl.BlockSpec((1, block, w), lambda bi, i: (bi, i, 0))
    par = pl.BlockSpec((1, w), lambda bi, i: (0, 0))
    return pl.pallas_call(
        functools.partial(_rwkv_chunk_body, chunk=chunk, n_chunks=block // chunk, n_pairs=n_pairs, group=group),
        grid=(b, nb),
        in_specs=[blk] * 7 + [par] * 3,
        out_specs=pl.BlockSpec((block, w), lambda bi, i: (bi * nb + i, 0)),
        out_shape=jax.ShapeDtypeStruct((b * s, w), BF16),
        scratch_shapes=[pltpu.VMEM((n_pairs, LANES, LANES), F32)],
        compiler_params=_params("parallel", "arbitrary"),
        name="rwkv_chunk",
    )(r, k, v, lw, kk, a, g, r_k.reshape(1, w), ln_w.reshape(1, w), ln_b.reshape(1, w))


def _hgrn_body(lbraw_ref, ng_ref, q_ref, f_ref, i_ref, g_ref, o_ref, st_ref, *, block, layer, group):
    @pl.when(pl.program_id(2) == 0)
    def _():
        st_ref[...] = jnp.zeros_like(st_ref)

    lbraw = lbraw_ref[...]
    e = jnp.exp(lbraw - jnp.max(lbraw, axis=0, keepdims=True))
    soft = e / jnp.sum(e, axis=0, keepdims=True)
    lb_all = jnp.sum(soft[:layer + 1], axis=0, keepdims=True) - soft[0:1]
    ng = ng_ref[...]

    c, sub = HG_CHUNK, HG_SUB
    n_sub = c // sub
    row = lax.broadcasted_iota(jnp.int32, (c, c), 0)
    col = lax.broadcasted_iota(jnp.int32, (c, c), 1)
    ones_lower = jnp.where(col <= row, 1.0, 0.0).astype(BF16)
    ones_sq = jnp.ones((LANES, LANES), BF16)
    ri = lax.broadcasted_iota(jnp.int32, (c, 1), 0)
    rs = lax.broadcasted_iota(jnp.int32, (sub, 1), 0)
    zeros_sub = jnp.zeros((sub, LANES), F32)
    off_row = lax.broadcasted_iota(jnp.int32, (c, (n_sub - 1) * c), 0) // sub
    off_col = lax.broadcasted_iota(jnp.int32, (c, (n_sub - 1) * c), 1) // c
    off_keep = off_row == off_col + 1
    lanes = [slice(h * LANES, (h + 1) * LANES) for h in range(group)]
    lb = [lb_all[:, ln] for ln in lanes]

    def each(fn, *lists):
        return [fn(*args) for args in zip(*lists)]

    def diag_pieces(q, k, b):
        pieces = []
        for i in range(n_sub):
            sl = slice(sub * i, sub * (i + 1))
            qi, ki, bi = q[sl], k[sl], b[sl] * LOG2_E
            for s in range(sub):
                decay = jnp.exp2(jnp.where(rs >= s, bi - bi[s:s + 1], NEG_BIG))
                pieces.append(qi * decay * ki[s:s + 1])
        return jnp.concatenate(pieces, axis=0)

    def diag_apply(sums, iv):
        parts = []
        for i in range(n_sub):
            od = zeros_sub
            for s in range(sub):
                r0 = (i * sub + s) * sub
                od = od + sums[r0:r0 + sub] * iv[sub * i + s:sub * i + s + 1]
            parts.append(od)
        return jnp.concatenate(parts, axis=0)

    def chunk_step(ci, carry):
        rows = pl.ds(pl.multiple_of(ci * c, c), c)
        q = [q_ref[rows, ln] for ln in lanes]
        iv = [i_ref[rows, ln] for ln in lanes]
        fg = each(lambda ln, lb_: lb_ + (1.0 - lb_) * _sigmoid(f_ref[rows, ln]), lanes, lb)
        k = each(lambda t: 1.0 - t, fg)
        b = each(lambda t: _cumsum_rows(ones_lower, jnp.log(t)), fg)
        st = [st_ref[h] for h in range(group)]
        o_inter = each(lambda q_, b_, s_: _dot_bf16(q_ * jnp.exp(b_), s_, _NT), q, b, st)

        def off_scores(q_, k_, b_):
            brefs = [b_[sub * i - 1:sub * i] for i in range(1, n_sub)]
            bref_rows = jnp.concatenate([zeros_sub] + [jnp.broadcast_to(t, (sub, LANES)) for t in brefs], axis=0)
            qh = q_ * jnp.exp(jnp.where(ri >= sub, b_ - bref_rows, NEG_BIG))
            kh = jnp.concatenate([k_ * jnp.exp(jnp.where(ri < sub * (i + 1), t - b_, NEG_BIG))
                                  for i, t in enumerate(brefs)], axis=0)
            return _dot_bf16(qh, kh, _NT)
        scores = each(off_scores, q, k, b)

        stacked = each(diag_pieces, q, k, b)
        sums = each(lambda t: jnp.dot(t.astype(BF16), ones_sq, preferred_element_type=F32), stacked)
        ivb = each(lambda t: t.astype(BF16), iv)
        o_off = each(lambda sc, v_: jnp.dot(jnp.where(off_keep, sc, 0.0).astype(BF16),
                                            jnp.concatenate([v_] * (n_sub - 1), axis=0),
                                            preferred_element_type=F32), scores, ivb)
        b_end = each(lambda t: t[c - 1:c], b)
        upd = each(lambda v_, k_, b_, e_: lax.dot_general(v_, (k_ * jnp.exp(e_ - b_)).astype(BF16), _TN,
                                                          preferred_element_type=F32), ivb, k, b, b_end)
        for h in range(group):
            st_ref[h] = st[h] * jnp.exp(b_end[h]) + upd[h]
            o = o_inter[h] + o_off[h] + diag_apply(sums[h], iv[h])
            o = o * lax.rsqrt(jnp.mean(o * o, axis=-1, keepdims=True) + EPS) * ng
            gg = g_ref[rows, lanes[h]]
            o_ref[rows, lanes[h]] = (o * (gg * _sigmoid(gg))).astype(o_ref.dtype)
        return carry

    lax.fori_loop(0, block // c, chunk_step, 0)


def _hgrn2(p, lb_raw, norm_g, batch, seq, layer, block=512, group=4):
    block = min(block, seq)
    nb = seq // block
    depth = lb_raw.shape[0]
    n_groups = HG_HEADS // group
    width = group * LANES
    col = lambda part: pl.BlockSpec((block, width), lambda b, h, i: (b * nb + i, part * n_groups + h))
    return pl.pallas_call(
        functools.partial(_hgrn_body, block=block, layer=layer, group=group),
        grid=(batch, n_groups, nb),
        in_specs=[pl.BlockSpec((depth, width), lambda b, h, i: (0, h)),
                  pl.BlockSpec((1, LANES), lambda b, h, i: (0, 0)),
                  col(0), col(1), col(2), col(3)],
        out_specs=pl.BlockSpec((block, width), lambda b, h, i: (b * nb + i, h)),
        out_shape=jax.ShapeDtypeStruct((batch * seq, D_MODEL), BF16),
        scratch_shapes=[pltpu.VMEM((group, LANES, LANES), F32)],
        compiler_params=_params("parallel", "parallel", "arbitrary"),
        name="hgrn2",
    )(lb_raw, norm_g.reshape(1, LANES), p, p, p, p)


def _ffn(x, norm_g, w_gate, w_up, w_down):
    h = _rmsnorm(x, norm_g, BF16)
    hidden = _ffn_up(h, w_gate.astype(BF16), w_up.astype(BF16))
    return _matmul_residual(hidden, w_down.astype(BF16), x)


def _even_layer(x, batch, seq, layer, norm_g, w_in, w_out, lam_vec, subln_g, mu, w0, w_up, a0, a_up, g_up,
                k_k, k_a, r_k, ln_w, ln_b):
    h = _rmsnorm(x, norm_g, BF16)
    w_in = w_in.astype(BF16)
    qk = _qk_proj(h, w_in[:, :2 * DA_WIDTH], seq)
    v = _matmul(h, w_in[:, 2 * DA_WIDTH:3 * DA_WIDTH], BF16)
    lam_init = 0.8 - 0.6 * math.exp(-0.3 * layer)
    a_out = _diff_attention(qk, v, lam_vec, subln_g, batch, seq, lam_init)

    rw0 = 3 * DA_WIDTH
    prw = _matmul(h, w_in[:, rw0:rw0 + 3 * RW_WIDTH], F32)
    w_lora = jnp.pad(w_in[:, rw0 + 3 * RW_WIDTH:], ((0, 0), (0, RW_LORA_PAD - RW_LORA)))
    plora = _matmul(h, w_lora, F32)
    tok = _rwkv_prep(prw.reshape(batch, seq, -1), plora.reshape(batch, seq, -1), mu, w0, w_up, a0, a_up, g_up,
                     k_k, k_a)
    b_out = _rwkv_chunked(*tok, r_k, ln_w, ln_b)

    w_out = w_out.astype(BF16)
    return _matmul2_residual(a_out, b_out, w_out[:DA_WIDTH], w_out[DA_WIDTH:], x)


def _odd_layer(x, batch, seq, layer, norm_g, w_in, w_out, lb_raw, hg_norm_g):
    h = _rmsnorm(x, norm_g, BF16)
    p = _matmul(h, w_in.astype(BF16), F32)
    o = _hgrn2(p, lb_raw, hg_norm_g, batch, seq, layer)
    return _matmul_residual(o, w_out.astype(BF16), x, tm=1024)


def kernel(x, attn_norm_g, ffn_norm_g, final_norm_g, even_w_in, even_w_out, da_lambda, da_subln_g, rw_mu, rw_w0, rw_w_up, rw_a0, rw_a_up, rw_g_up, rw_k_k, rw_k_a, rw_r_k, rw_ln_w, rw_ln_b, odd_w_in, odd_w_out, hg_lower_bound, hg_norm_g, ffn_w_gate, ffn_w_up, ffn_w_down):
    batch, seq, d = x.shape
    depth = attn_norm_g.shape[0]
    xs = x.reshape(batch * seq, d)
    for layer in range(depth):
        j = layer // 2
        if layer % 2 == 0:
            xs = _even_layer(xs, batch, seq, layer, attn_norm_g[layer], even_w_in[j], even_w_out[j], da_lambda[j],
                             da_subln_g[j], rw_mu[j], rw_w0[j], rw_w_up[j], rw_a0[j], rw_a_up[j], rw_g_up[j],
                             rw_k_k[j], rw_k_a[j], rw_r_k[j], rw_ln_w[j], rw_ln_b[j])
        else:
            xs = _odd_layer(xs, batch, seq, layer, attn_norm_g[layer], odd_w_in[j], odd_w_out[j],
                            hg_lower_bound, hg_norm_g[j])
        xs = _ffn(xs, ffn_norm_g[layer], ffn_w_gate[layer], ffn_w_up[layer], ffn_w_down[layer])
    return _rmsnorm(xs, final_norm_g, F32).reshape(batch, seq, d)
```

```python
import functools
import math

import jax
import jax.numpy as jnp
from jax import lax
from jax.experimental import pallas as pl
from jax.experimental.pallas import tpu as pltpu

D_MODEL = 2048
EPS = 1e-6

DA_WIDTH = D_MODEL // 2
DA_HEADS = 8
DA_HEAD_DIM = DA_WIDTH // DA_HEADS // 2
DA_V_DIM = 2 * DA_HEAD_DIM
ROPE_DIM = DA_HEAD_DIM // 4
ROPE_THETA = 500000.0

RW_WIDTH = D_MODEL // 2
RW_HEAD = 64
RW_HEADS = RW_WIDTH // RW_HEAD
DECAY_LORA = 64
AAA_LORA = 64
GATE_LORA = 32
RW_LORA = DECAY_LORA + AAA_LORA + GATE_LORA
RW_LORA_PAD = 256
RW_GN_EPS = 64e-5
RW_CHUNK = 64

HG_EXPAND = 128
HG_HEADS = D_MODEL // HG_EXPAND
HG_CHUNK = 64
HG_SUB = 8

LANES = 128
NEG_BIG = -1e30
LOG2_E = 1.0 / math.log(2.0)
VMEM_LIMIT = 48 * 1024 * 1024

F32 = jnp.float32
BF16 = jnp.bfloat16

_NN = (((1,), (0,)), ((), ()))
_NT = (((1,), (1,)), ((), ()))
_TN = (((0,), (0,)), ((), ()))


def _dot_bf16(a, b, dims=_NN):
    return lax.dot_general(a.astype(BF16), b.astype(BF16), dims, preferred_element_type=F32)


def _sigmoid(x):
    return 1.0 / (1.0 + jnp.exp(-x))


def _params(*sem):
    return pltpu.CompilerParams(dimension_semantics=sem, vmem_limit_bytes=VMEM_LIMIT)


def _rmsnorm_rows(x, g):
    return x * lax.rsqrt(jnp.mean(x * x, axis=-1, keepdims=True) + EPS) * g


def _fill_normed(x_ref, g_ref, h_ref):
    @pl.when(pl.program_id(1) == 0)
    def _():
        h_ref[...] = _rmsnorm_rows(x_ref[...], g_ref[...]).astype(BF16)


def _final_norm_body(x_ref, g_ref, o_ref):
    o_ref[...] = _rmsnorm_rows(x_ref[...], g_ref[...])


def _final_norm(x2d, g, tm=512):
    t, d = x2d.shape
    tm = min(tm, t)
    return pl.pallas_call(
        _final_norm_body,
        grid=(t // tm,),
        in_specs=[pl.BlockSpec((tm, d), lambda i: (i, 0)), pl.BlockSpec((1, d), lambda i: (0, 0))],
        out_specs=pl.BlockSpec((tm, d), lambda i: (i, 0)),
        out_shape=jax.ShapeDtypeStruct((t, d), F32),
        compiler_params=_params("parallel"),
        name="final_norm",
    )(x2d, g.reshape(1, d))


def _norm_mm_body(x_ref, g_ref, w_ref, o_ref, h_ref):
    _fill_normed(x_ref, g_ref, h_ref)
    o_ref[...] = jnp.dot(h_ref[...], w_ref[...].astype(BF16), preferred_element_type=F32).astype(o_ref.dtype)


def _norm_matmul(x, g, w, layer, col0, n, out_dtype, tm=1024, tn=512):
    m, k = x.shape
    tm, tn = min(tm, m), min(tn, n)
    c0 = col0 // tn
    return pl.pallas_call(
        _norm_mm_body,
        grid=(m // tm, n // tn),
        in_specs=[pl.BlockSpec((tm, k), lambda i, j: (i, 0)), pl.BlockSpec((1, k), lambda i, j: (0, 0)),
                  pl.BlockSpec((None, k, tn), lambda i, j: (layer, 0, c0 + j))],
        out_specs=pl.BlockSpec((tm, tn), lambda i, j: (i, j)),
        out_shape=jax.ShapeDtypeStruct((m, n), out_dtype),
        scratch_shapes=[pltpu.VMEM((tm, k), BF16)],
        compiler_params=_params("parallel", "arbitrary"),
        name="norm_matmul",
    )(x, g.reshape(1, k), w)


def _mm_res_body(a_ref, w_ref, r_ref, o_ref):
    o_ref[...] = r_ref[...] + jnp.dot(a_ref[...], w_ref[...].astype(BF16), preferred_element_type=F32)


def _matmul_residual(a, w, layer, res, tm=1024, tn=256):
    m, k = a.shape
    n = w.shape[-1]
    tm, tn = min(tm, m), min(tn, n)
    return pl.pallas_call(
        _mm_res_body,
        grid=(m // tm, n // tn),
        in_specs=[
            pl.BlockSpec((tm, k), lambda i, j: (i, 0)),
            pl.BlockSpec((None, k, tn), lambda i, j: (layer, 0, j)),
            pl.BlockSpec((tm, tn), lambda i, j: (i, j)),
        ],
        out_specs=pl.BlockSpec((tm, tn), lambda i, j: (i, j)),
        out_shape=jax.ShapeDtypeStruct((m, n), F32),
        compiler_params=_params("parallel", "arbitrary"),
        name="matmul_residual",
    )(a, w, res)


def _mm2_res_body(a1_ref, a2_ref, w1_ref, w2_ref, r_ref, o_ref):
    acc = jnp.dot(a1_ref[...], w1_ref[...].astype(BF16), preferred_element_type=F32)
    acc = acc + jnp.dot(a2_ref[...], w2_ref[...].astype(BF16), preferred_element_type=F32)
    o_ref[...] = r_ref[...] + acc


def _matmul2_residual(a1, a2, w, layer, res, tm=2048, tn=256):
    m, k1 = a1.shape
    n = w.shape[-1]
    tm, tn = min(tm, m), min(tn, n)
    return pl.pallas_call(
        _mm2_res_body,
        grid=(m // tm, n // tn),
        in_specs=[
            pl.BlockSpec((tm, k1), lambda i, j: (i, 0)),
            pl.BlockSpec((tm, k1), lambda i, j: (i, 0)),
            pl.BlockSpec((None, k1, tn), lambda i, j: (layer, 0, j)),
            pl.BlockSpec((None, k1, tn), lambda i, j: (layer, 1, j)),
            pl.BlockSpec((tm, tn), lambda i, j: (i, j)),
        ],
        out_specs=pl.BlockSpec((tm, tn), lambda i, j: (i, j)),
        out_shape=jax.ShapeDtypeStruct((m, n), F32),
        compiler_params=_params("parallel", "arbitrary"),
        name="out_proj_residual",
    )(a1, a2, w, w, res)


def _ffn_up_body(x_ref, g_ref, wg_ref, wu_ref, o_ref, h_ref):
    _fill_normed(x_ref, g_ref, h_ref)
    h = h_ref[...]
    gate = jnp.dot(h, wg_ref[...].astype(BF16), preferred_element_type=F32)
    up = jnp.dot(h, wu_ref[...].astype(BF16), preferred_element_type=F32)
    o_ref[...] = (gate * _sigmoid(gate) * up).astype(o_ref.dtype)


def _ffn_up(x, g, wg, wu, layer, tm=1024, tn=512):
    m, k = x.shape
    n = wg.shape[-1]
    tm, tn = min(tm, m), min(tn, n)
    wspec = pl.BlockSpec((None, k, tn), lambda i, j: (layer, 0, j))
    return pl.pallas_call(
        _ffn_up_body,
        grid=(m // tm, n // tn),
        in_specs=[pl.BlockSpec((tm, k), lambda i, j: (i, 0)), pl.BlockSpec((1, k), lambda i, j: (0, 0)),
                  wspec, wspec],
        out_specs=pl.BlockSpec((tm, tn), lambda i, j: (i, j)),
        out_shape=jax.ShapeDtypeStruct((m, n), BF16),
        scratch_shapes=[pltpu.VMEM((tm, k), BF16)],
        compiler_params=_params("parallel", "arbitrary"),
        name="ffn_up",
    )(x, g.reshape(1, k), wg, wu)


def _qk_rope_body(x_ref, g_ref, w_ref, cos_ref, sin_lo_ref, sin_hi_ref, o_ref, h_ref, *, n_q_tiles, tn, q_scale):
    _fill_normed(x_ref, g_ref, h_ref)
    acc = jnp.dot(h_ref[...], w_ref[...].astype(BF16), preferred_element_type=F32)
    scale = jnp.where(pl.program_id(1) < n_q_tiles, q_scale, 1.0).astype(F32)
    cos = cos_ref[...]
    sin_lo = sin_lo_ref[...]
    sin_hi = sin_hi_ref[...]
    half = ROPE_DIM // 2
    for g in range(tn // LANES):
        x = acc[:, g * LANES:(g + 1) * LANES]
        y = x * cos + pltpu.roll(x, half, 1) * sin_hi + pltpu.roll(x, LANES - half, 1) * sin_lo
        o_ref[:, g * LANES:(g + 1) * LANES] = (y * scale).astype(o_ref.dtype)


def _rope_tables(seq):
    pos = jnp.arange(seq, dtype=F32)
    inv_freq = 1.0 / (ROPE_THETA ** (jnp.arange(0, ROPE_DIM, 2, dtype=F32) / ROPE_DIM))
    ang = pos[:, None] * inv_freq[None, :]
    cos, sin = jnp.cos(ang), jnp.sin(ang)
    half = ROPE_DIM // 2
    pad = jnp.zeros((seq, DA_HEAD_DIM - ROPE_DIM), F32)
    zer = jnp.zeros((seq, half), F32)
    cos_c = jnp.concatenate([cos, cos, pad + 1.0], axis=1)
    sin_lo = jnp.concatenate([-sin, zer, pad], axis=1)
    sin_hi = jnp.concatenate([zer, sin, pad], axis=1)
    rep = lambda t: jnp.concatenate([t, t], axis=1)
    return rep(cos_c), rep(sin_lo), rep(sin_hi)


def _qk_proj(x, g, w_in, layer, seq, tm=1024, tn=512):
    m, k = x.shape
    n = 2 * DA_WIDTH
    tm, tn = min(tm, seq), min(tn, n)
    cos, sin_lo, sin_hi = _rope_tables(seq)
    s_tiles = seq // tm
    body = functools.partial(_qk_rope_body, n_q_tiles=DA_WIDTH // tn, tn=tn, q_scale=DA_HEAD_DIM ** -0.5 * LOG2_E)
    tab = pl.BlockSpec((tm, LANES), lambda i, j: (i % s_tiles, 0))
    return pl.pallas_call(
        body,
        grid=(m // tm, n // tn),
        in_specs=[pl.BlockSpec((tm, k), lambda i, j: (i, 0)), pl.BlockSpec((1, k), lambda i, j: (0, 0)),
                  pl.BlockSpec((None, k, tn), lambda i, j: (layer, 0, j)), tab, tab, tab],
        out_specs=pl.BlockSpec((tm, tn), lambda i, j: (i, j)),
        out_shape=jax.ShapeDtypeStruct((m, n), BF16),
        scratch_shapes=[pltpu.VMEM((tm, k), BF16)],
        compiler_params=_params("parallel", "arbitrary"),
        name="qk_proj_rope",
    )(x, g.reshape(1, k), w_in, cos, sin_lo, sin_hi)


def _attn_body(lam_ref, g_ref, q_ref, k_ref, v_ref, o_ref, *, tq, tk, lam_init, group):
    qi = pl.program_id(2)
    lane = lax.broadcasted_iota(jnp.int32, (1, LANES), 1)
    chains = []
    for h in range(group):
        ln = slice(h * LANES, (h + 1) * LANES)
        q = q_ref[:, ln]
        zero = jnp.zeros_like(q)
        chains.append((ln, jnp.where(lane < DA_HEAD_DIM, q, zero)))
        chains.append((ln, jnp.where(lane >= DA_HEAD_DIM, q, zero)))

    def score(c, rows):
        ln, q = chains[c]
        return lax.dot_general(k_ref[rows, ln], q, _NT, preferred_element_type=F32)

    def scores(rows):
        return tuple(score(c, rows) for c in range(len(chains)))

    def absorb(rows, s, carry, next_rows=None):
        n = len(chains)
        s_next = [None] * n
        if next_rows is not None:
            s_next[0] = score(0, next_rows)
        pending = []
        for c in range(n):
            if next_rows is not None and c + 1 < n:
                s_next[c + 1] = score(c + 1, next_rows)
            m = carry[c][0]
            m_new = jnp.maximum(m, jnp.max(s[c], axis=0, keepdims=True))
            p = jnp.exp2(s[c] - m_new)
            pv = lax.dot_general(v_ref[rows, chains[c][0]], p.astype(BF16), _TN, preferred_element_type=F32)
            pending.append((m_new, jnp.exp2(m - m_new), jnp.sum(p, axis=0, keepdims=True), pv))
        out = tuple((mn, alpha * l + ps, alpha * acc + pv)
                    for (_, l, acc), (mn, alpha, ps, pv) in zip(carry, pending))
        return tuple(s_next), out

    def block(j):
        return pl.ds(pl.multiple_of(j * tk, tk), tk)

    init = tuple((jnp.full((1, tq), NEG_BIG, F32), jnp.zeros((1, tq), F32), jnp.zeros((LANES, tq), F32))
                 for _ in chains)
    per_q = tq // tk
    n_full = qi * per_q

    def pipelined(j, state):
        s_cur, carry = state
        return absorb(block(j), s_cur, carry, next_rows=block(j + 1))
    s, carry = lax.fori_loop(0, n_full, pipelined, (scores(block(0)), init))

    for d in range(per_q):
        key = lax.broadcasted_iota(jnp.int32, (tk, tq), 0) + d * tk
        keep = key <= lax.broadcasted_iota(jnp.int32, (tk, tq), 1)
        if d > 0:
            s = scores(block(n_full + d))
        _, carry = absorb(block(n_full + d), [jnp.where(keep, t, NEG_BIG) for t in s], carry)

    lf = lam_ref[...]
    lam = (jnp.exp(jnp.sum(lf[0:1] * lf[1:2], axis=-1, keepdims=True))
           - jnp.exp(jnp.sum(lf[2:3] * lf[3:4], axis=-1, keepdims=True)) + lam_init)
    for h in range(group):
        (ln, _), (_, l0, acc0), (_, l1, acc1) = chains[2 * h], carry[2 * h], carry[2 * h + 1]
        o = acc0 / l0 - lam * (acc1 / l1)
        o = o * lax.rsqrt(jnp.mean(o * o, axis=0, keepdims=True) + EPS)
        o_ref[:, ln] = (o.T * g_ref[...] * (1.0 - lam_init)).astype(o_ref.dtype)


def _diff_attention(qk, v, lam_vec, subln_g, batch, seq, lam_init, tq=256, tk=256, group=4):
    tq = min(tq, seq)
    tk = min(tk, tq)
    nq = seq // tq
    n_groups = DA_HEADS // group
    width = group * LANES
    body = functools.partial(_attn_body, tq=tq, tk=tk, lam_init=lam_init, group=group)
    return pl.pallas_call(
        body,
        grid=(batch, n_groups, nq),
        in_specs=[
            pl.BlockSpec((4, DA_HEAD_DIM), lambda b, h, i: (0, 0)),
            pl.BlockSpec((1, DA_V_DIM), lambda b, h, i: (0, 0)),
            pl.BlockSpec((tq, width), lambda b, h, i: (b * nq + i, h)),
            pl.BlockSpec((seq, width), lambda b, h, i: (b, n_groups + h)),
            pl.BlockSpec((seq, width), lambda b, h, i: (b, h)),
        ],
        out_specs=pl.BlockSpec((tq, width), lambda b, h, i: (b * nq + i, h)),
        out_shape=jax.ShapeDtypeStruct((batch * seq, DA_WIDTH), BF16),
        compiler_params=_params("parallel", "parallel", "arbitrary"),
        name="diff_attention",
    )(lam_vec, subln_g.reshape(1, DA_V_DIM), qk, qk, v)


def _rwkv_prep_body(x_ref, xprev_ref, l_ref, lprev_ref, mu_ref, mul_ref, w0_ref, a0_ref, kk_ref, ka_ref,
                    wup_ref, aup_ref, gup_ref,
                    r_out, k_out, v_out, lw_out, kk_out, a_out, g_out, *, ts):
    first = pl.program_id(1) == 0
    row = lax.broadcasted_iota(jnp.int32, (ts, 1), 0)

    def token_shift(cur_ref, prev_ref, mu):
        cur = cur_ref[0]
        last = jnp.where(first, 0.0, prev_ref[0][7:8, :])
        prev = jnp.where(row == 0, last, pltpu.roll(cur, 1, 0))
        return cur + (prev - cur) * mu

    xm = token_shift(x_ref, xprev_ref, mu_ref[...])
    xl = token_shift(l_ref, lprev_ref, mul_ref[...])
    r = xm[:, :RW_WIDTH]
    k = xm[:, RW_WIDTH:2 * RW_WIDTH]
    v = xm[:, 2 * RW_WIDTH:]
    z = -(w0_ref[...] + _dot_bf16(jnp.tanh(xl), wup_ref[...]))
    softplus = jnp.maximum(z, 0.0) + jnp.log(1.0 + jnp.exp(-jnp.abs(z)))
    w_log = -softplus - 0.5
    a = _sigmoid(a0_ref[...] + _dot_bf16(xl, aup_ref[...]))
    g = _dot_bf16(_sigmoid(xl), gup_ref[...])
    r_out[0] = r
    k_out[0] = k * (1.0 + (a - 1.0) * ka_ref[...])
    v_out[0] = v
    lw_out[0] = -jnp.exp(w_log)
    kk_out[0] = k * kk_ref[...]
    a_out[0] = a
    g_out[0] = g


def _rwkv_prep(prw, plora, mu, w0, w_up, a0, a_up, g_up, k_k, k_a, ts=256):
    b, s, w3 = prw.shape
    ts = min(ts, s)
    wl = plora.shape[-1]
    pad_rows = lambda m, lo: jnp.zeros((wl, RW_WIDTH), F32).at[lo:lo + m.shape[0]].set(m)
    wup = pad_rows(w_up, 0)
    aup = pad_rows(a_up, DECAY_LORA)
    gup = pad_rows(g_up, DECAY_LORA + AAA_LORA)
    mu_main = mu[:w3].reshape(1, w3)
    mu_lora = jnp.zeros((1, wl), F32).at[0, :RW_LORA].set(mu[w3:])
    row = lambda t: t.reshape(1, RW_WIDTH)
    blk = lambda w: pl.BlockSpec((1, ts, w), lambda bi, i: (bi, i, 0))
    prev = lambda w: pl.BlockSpec((1, 8, w), lambda bi, i: (bi, jnp.maximum(i * (ts // 8) - 1, 0), 0))
    full = lambda r, c: pl.BlockSpec((r, c), lambda bi, i: (0, 0))
    out_sds = jax.ShapeDtypeStruct((b, s, RW_WIDTH), F32)
    return pl.pallas_call(
        functools.partial(_rwkv_prep_body, ts=ts),
        grid=(b, s // ts),
        in_specs=[blk(w3), prev(w3), blk(wl), prev(wl), full(1, w3), full(1, wl),
                  full(1, RW_WIDTH), full(1, RW_WIDTH), full(1, RW_WIDTH), full(1, RW_WIDTH),
                  full(wl, RW_WIDTH), full(wl, RW_WIDTH), full(wl, RW_WIDTH)],
        out_specs=[blk(RW_WIDTH)] * 7,
        out_shape=[out_sds] * 7,
        compiler_params=_params("parallel", "parallel"),
        name="rwkv_prep",
    )(prw, prw, plora, plora, mu_main, mu_lora, row(w0), row(a0), row(k_k), row(k_a), wup, aup, gup)


def _cumsum_rows(ones_lower_bf16, x):
    hi = x.astype(BF16)
    rem = x - hi.astype(F32)
    mid = rem.astype(BF16)
    lo = (rem - mid.astype(F32)).astype(BF16)
    width = x.shape[1]
    out = jnp.dot(ones_lower_bf16, jnp.concatenate([hi, mid, lo], axis=1), preferred_element_type=F32)
    return out[:, :width] + out[:, width:2 * width] + out[:, 2 * width:]


def _rwkv_chunk_body(r_ref, k_ref, v_ref, lw_ref, kk_ref, a_ref, g_ref, rk_ref, lnw_ref, lnb_ref,
                     o_ref, st_ref, *, chunk, n_chunks, n_pairs, group):
    @pl.when(pl.program_id(1) == 0)
    def _():
        st_ref[...] = jnp.zeros_like(st_ref)

    c = chunk
    head = RW_HEAD
    lane = lax.broadcasted_iota(jnp.int32, (1, LANES), 1)
    head0 = lane < head
    row_c = lax.broadcasted_iota(jnp.int32, (c, LANES), 0)
    col_c = lax.broadcasted_iota(jnp.int32, (c, LANES), 1)
    col_in_head = jnp.where(col_c < head, col_c, col_c - head)
    strict = col_in_head < row_c
    lower = col_in_head <= row_c
    row_p = lax.broadcasted_iota(jnp.int32, (LANES, LANES), 0)
    col_p = lax.broadcasted_iota(jnp.int32, (LANES, LANES), 1)
    eye = jnp.where(row_p == col_p, 1.0, 0.0).astype(F32)
    same_head = (row_p < head) == (col_p < head)
    ones_lower = jnp.where(lax.broadcasted_iota(jnp.int32, (c, c), 1) <= lax.broadcasted_iota(jnp.int32, (c, c), 0),
                           1.0, 0.0).astype(BF16)
    zeros_c = jnp.zeros((c, LANES), BF16)
    n_doublings = int(math.log2(c)) - 1

    def only0(x):
        return jnp.where(head0, x, jnp.zeros_like(x))

    def only1(x):
        return jnp.where(head0, jnp.zeros_like(x), x)

    def head_sum(x):
        s0 = jnp.sum(only0(x), axis=-1, keepdims=True)
        s1 = jnp.sum(only1(x), axis=-1, keepdims=True)
        return jnp.where(head0, s0, s1)

    def each(fn, *lists):
        return [fn(*args) for args in zip(*lists)]

    def chunk_group(rows, pairs):
        lanes = [slice(p * LANES, (p + 1) * LANES) for p in pairs]
        load = lambda ref: [ref[0, rows, ln] for ln in lanes]
        r, k, v, lw, kkr, ag = load(r_ref), load(k_ref), load(v_ref), load(lw_ref), load(kk_ref), load(a_ref)
        kk = each(lambda t: t / jnp.maximum(jnp.sqrt(head_sum(t * t)), 1e-12), kkr)
        bm = each(lambda a, b: a * b, kk, ag)
        cw = each(lambda t: _cumsum_rows(ones_lower, t), lw)
        w_inv = each(lambda t: jnp.exp(-t), cw)
        b_t = each(lambda a, b: (a * b).astype(BF16), bm, w_inv)
        k_t = each(lambda a, b: (a * b).astype(BF16), k, w_inv)
        lhs = each(lambda kk_, cw_, lw_, r_: jnp.concatenate([-kk_ * jnp.exp(cw_ - lw_), r_ * jnp.exp(cw_)],
                                                             axis=0).astype(BF16), kk, cw, lw, r)
        g0 = each(lambda l, b, k_: lax.dot_general(only0(l), jnp.concatenate([b, k_], axis=0), _NT,
                                                   preferred_element_type=F32), lhs, b_t, k_t)
        g1 = each(lambda l, b, k_: lax.dot_general(only1(l), jnp.concatenate([k_, b], axis=0), _NT,
                                                   preferred_element_type=F32), lhs, b_t, k_t)
        top0 = each(lambda g_: jnp.where(strict, g_[:c], 0.0), g0)
        top1 = each(lambda g_: jnp.where(strict, g_[:c], 0.0), g1)
        bot = each(lambda a, b: jnp.concatenate([jnp.where(lower, a[c:], 0.0), jnp.where(lower, b[c:], 0.0)],
                                                axis=1).astype(BF16), g0, g1)
        power = each(lambda a, b: jnp.concatenate([only0(a), only1(b)], axis=0), top0, top1)
        inv = each(lambda n: eye + n, power)
        for _ in range(n_doublings):
            power = each(lambda t: jnp.dot(t.astype(BF16), t.astype(BF16), preferred_element_type=F32), power)
            inv = each(lambda t, pw: t + jnp.dot(t.astype(BF16), pw.astype(BF16), preferred_element_type=F32),
                       inv, power)
        st = [st_ref[p] for p in pairs]
        from_state = each(lambda l, s_: lax.dot_general(l, s_.astype(BF16), _NT, preferred_element_type=F32),
                          lhs, st)
        vb = each(lambda t: t.astype(BF16), v)
        x = each(lambda fs, a, b, v_: fs[:c] + jnp.dot(
            jnp.concatenate([a, b], axis=1).astype(BF16),
            jnp.concatenate([zeros_c, only0(v_), only1(v_), zeros_c], axis=0), preferred_element_type=F32),
            from_state, top0, top1, vb)
        tu = each(lambda t, x_: jnp.dot(t.astype(BF16), jnp.concatenate([only0(x_), only1(x_)], axis=0).astype(BF16),
                                        preferred_element_type=F32), inv, x)
        ub = each(lambda t: (t[:c] + t[c:]).astype(BF16), tu)
        y = each(lambda fs, b, u_, v_: fs[c:] + jnp.dot(
            b, jnp.concatenate([only0(u_), only0(v_), only1(v_), only1(u_)], axis=0), preferred_element_type=F32),
            from_state, bot, ub, vb)
        w_end = each(lambda t: t[c - 1:c], cw)
        tail = each(lambda e, t: jnp.exp(e - t), w_end, cw)
        upd = each(lambda u_, v_, bm_, k_, tl: lax.dot_general(
            jnp.concatenate([u_, v_], axis=0),
            jnp.concatenate([(bm_ * tl).astype(BF16), (k_ * tl).astype(BF16)], axis=0), _TN,
            preferred_element_type=F32), ub, vb, bm, k, tail)
        for p, s_, e, up in zip(pairs, st, w_end, upd):
            st_ref[p] = s_ * jnp.exp(e) + jnp.where(same_head, up, 0.0)
        for ln, y_, r_, k_, v_ in zip(lanes, y, r, k, v):
            yc = y_ - head_sum(y_) * (1.0 / head)
            var = head_sum(yc * yc) * (1.0 / head)
            yn = yc * lax.rsqrt(var + RW_GN_EPS) * lnw_ref[:, ln] + lnb_ref[:, ln]
            bonus = head_sum(r_ * k_ * rk_ref[:, ln]) * v_
            o_ref[rows, ln] = ((yn + bonus) * g_ref[0, rows, ln]).astype(o_ref.dtype)

    for ci in range(n_chunks):
        for p0 in range(0, n_pairs, group):
            chunk_group(slice(ci * c, (ci + 1) * c), list(range(p0, min(p0 + group, n_pairs))))


def _rwkv_chunked(r, k, v, lw, kk, a, g, r_k, ln_w, ln_b, chunk=RW_CHUNK, block=RW_CHUNK, group=8):
    b, s, w = r.shape
    chunk = min(chunk, s)
    block = min(block, s)
    nb = s // block
    n_pairs = w // LANES
    blk = pl.BlockSpec((1, block, w), lambda bi, i: (bi, i, 0))
    par = pl.BlockSpec((1, w), lambda bi, i: (0, 0))
    return pl.pallas_call(
        functools.partial(_rwkv_chunk_body, chunk=chunk, n_chunks=block // chunk, n_pairs=n_pairs, group=group),
        grid=(b, nb),
        in_specs=[blk] * 7 + [par] * 3,
        out_specs=pl.BlockSpec((block, w), lambda bi, i: (bi * nb + i, 0)),
        out_shape=jax.ShapeDtypeStruct((b * s, w), BF16),
        scratch_shapes=[pltpu.VMEM((n_pairs, LANES, LANES), F32)],
        compiler_params=_params("parallel", "arbitrary"),
        name="rwkv_chunk",
    )(r, k, v, lw, kk, a, g, r_k.reshape(1, w), ln_w.reshape(1, w), ln_b.reshape(1, w))


def _hgrn_body(lbraw_ref, ng_ref, q_ref, f_ref, i_ref, g_ref, o_ref, st_ref, *, block, layer, group):
    @pl.when(pl.program_id(2) == 0)
    def _():
        st_ref[...] = jnp.zeros_like(st_ref)

    lbraw = lbraw_ref[...]
    e = jnp.exp(lbraw - jnp.max(lbraw, axis=0, keepdims=True))
    soft = e / jnp.sum(e, axis=0, keepdims=True)
    lb_all = jnp.sum(soft[:layer + 1], axis=0, keepdims=True) - soft[0:1]
    ng = ng_ref[...]

    c, sub = HG_CHUNK, HG_SUB
    n_sub = c // sub
    row = lax.broadcasted_iota(jnp.int32, (c, c), 0)
    col = lax.broadcasted_iota(jnp.int32, (c, c), 1)
    ones_lower = jnp.where(col <= row, 1.0, 0.0).astype(BF16)
    ones_sq = jnp.ones((LANES, LANES), BF16)
    ri = lax.broadcasted_iota(jnp.int32, (c, 1), 0)
    rs = lax.broadcasted_iota(jnp.int32, (sub, 1), 0)
    zeros_sub = jnp.zeros((sub, LANES), F32)
    off_row = lax.broadcasted_iota(jnp.int32, (c, (n_sub - 1) * c), 0) // sub
    off_col = lax.broadcasted_iota(jnp.int32, (c, (n_sub - 1) * c), 1) // c
    off_keep = off_row == off_col + 1
    lanes = [slice(h * LANES, (h + 1) * LANES) for h in range(group)]
    lb = [lb_all[:, ln] for ln in lanes]

    def each(fn, *lists):
        return [fn(*args) for args in zip(*lists)]

    def diag_pieces(q, k, b):
        pieces = []
        for i in range(n_sub):
            sl = slice(sub * i, sub * (i + 1))
            qi, ki, bi = q[sl], k[sl], b[sl] * LOG2_E
            for s in range(sub):
                decay = jnp.exp2(jnp.where(rs >= s, bi - bi[s:s + 1], NEG_BIG))
                pieces.append(qi * decay * ki[s:s + 1])
        return jnp.concatenate(pieces, axis=0)

    def diag_apply(sums, iv):
        parts = []
        for i in range(n_sub):
            od = zeros_sub
            for s in range(sub):
                r0 = (i * sub + s) * sub
                od = od + sums[r0:r0 + sub] * iv[sub * i + s:sub * i + s + 1]
            parts.append(od)
        return jnp.concatenate(parts, axis=0)

    def chunk_step(ci, carry):
        rows = pl.ds(pl.multiple_of(ci * c, c), c)
        q = [q_ref[rows, ln] for ln in lanes]
        iv = [i_ref[rows, ln] for ln in lanes]
        fg = each(lambda ln, lb_: lb_ + (1.0 - lb_) * _sigmoid(f_ref[rows, ln]), lanes, lb)
        k = each(lambda t: 1.0 - t, fg)
        b = each(lambda t: _cumsum_rows(ones_lower, jnp.log(t)), fg)
        st = [st_ref[h] for h in range(group)]
        o_inter = each(lambda q_, b_, s_: _dot_bf16(q_ * jnp.exp(b_), s_, _NT), q, b, st)

        def off_scores(q_, k_, b_):
            brefs = [b_[sub * i - 1:sub * i] for i in range(1, n_sub)]
            bref_rows = jnp.concatenate([zeros_sub] + [jnp.broadcast_to(t, (sub, LANES)) for t in brefs], axis=0)
            qh = q_ * jnp.exp(jnp.where(ri >= sub, b_ - bref_rows, NEG_BIG))
            kh = jnp.concatenate([k_ * jnp.exp(jnp.where(ri < sub * (i + 1), t - b_, NEG_BIG))
                                  for i, t in enumerate(brefs)], axis=0)
            return _dot_bf16(qh, kh, _NT)
        scores = each(off_scores, q, k, b)

        stacked = each(diag_pieces, q, k, b)
        sums = each(lambda t: jnp.dot(t.astype(BF16), ones_sq, preferred_element_type=F32), stacked)
        ivb = each(lambda t: t.astype(BF16), iv)
        o_off = each(lambda sc, v_: jnp.dot(jnp.where(off_keep, sc, 0.0).astype(BF16),
                                            jnp.concatenate([v_] * (n_sub - 1), axis=0),
                                            preferred_element_type=F32), scores, ivb)
        b_end = each(lambda t: t[c - 1:c], b)
        upd = each(lambda v_, k_, b_, e_: lax.dot_general(v_, (k_ * jnp.exp(e_ - b_)).astype(BF16), _TN,
                                                          preferred_element_type=F32), ivb, k, b, b_end)
        for h in range(group):
            st_ref[h] = st[h] * jnp.exp(b_end[h]) + upd[h]
            o = o_inter[h] + o_off[h] + diag_apply(sums[h], iv[h])
            o = o * lax.rsqrt(jnp.mean(o * o, axis=-1, keepdims=True) + EPS) * ng
            gg = g_ref[rows, lanes[h]]
            o_ref[rows, lanes[h]] = (o * (gg * _sigmoid(gg))).astype(o_ref.dtype)
        return carry

    lax.fori_loop(0, block // c, chunk_step, 0, unroll=8)


def _hgrn2(p, lb_raw, norm_g, batch, seq, layer, block=512, group=4):
    block = min(block, seq)
    nb = seq // block
    depth = lb_raw.shape[0]
    n_groups = HG_HEADS // group
    width = group * LANES
    col = lambda part: pl.BlockSpec((block, width), lambda b, h, i: (b * nb + i, part * n_groups + h))
    return pl.pallas_call(
        functools.partial(_hgrn_body, block=block, layer=layer, group=group),
        grid=(batch, n_groups, nb),
        in_specs=[pl.BlockSpec((depth, width), lambda b, h, i: (0, h)),
                  pl.BlockSpec((1, LANES), lambda b, h, i: (0, 0)),
                  col(0), col(1), col(2), col(3)],
        out_specs=pl.BlockSpec((block, width), lambda b, h, i: (b * nb + i, h)),
        out_shape=jax.ShapeDtypeStruct((batch * seq, D_MODEL), BF16),
        scratch_shapes=[pltpu.VMEM((group, LANES, LANES), F32)],
        compiler_params=_params("parallel", "parallel", "arbitrary"),
        name="hgrn2",
    )(lb_raw, norm_g.reshape(1, LANES), p, p, p, p)


def _ffn(x, layer, norm_g, w_gate, w_up, w_down):
    hidden = _ffn_up(x, norm_g, w_gate, w_up, layer)
    return _matmul_residual(hidden, w_down, layer, x)


def _even_layer(x, batch, seq, layer, j, norm_g, w_in, w_out, lam_vec, subln_g, mu, w0, w_up, a0, a_up, g_up,
                k_k, k_a, r_k, ln_w, ln_b):
    qk = _qk_proj(x, norm_g, w_in, j, seq)
    v = _norm_matmul(x, norm_g, w_in, j, 2 * DA_WIDTH, DA_WIDTH, BF16)
    lam_init = 0.8 - 0.6 * math.exp(-0.3 * layer)
    a_out = _diff_attention(qk, v, lam_vec, subln_g, batch, seq, lam_init)

    rw0 = 3 * DA_WIDTH
    prw = _norm_matmul(x, norm_g, w_in, j, rw0, 3 * RW_WIDTH, F32)
    w_lora = jnp.pad(w_in[j, :, rw0 + 3 * RW_WIDTH:], ((0, 0), (0, RW_LORA_PAD - RW_LORA)))
    plora = _norm_matmul(x, norm_g, w_lora[None], 0, 0, RW_LORA_PAD, F32)
    tok = _rwkv_prep(prw.reshape(batch, seq, -1), plora.reshape(batch, seq, -1), mu, w0, w_up, a0, a_up, g_up,
                     k_k, k_a)
    b_out = _rwkv_chunked(*tok, r_k, ln_w, ln_b)
    return _matmul2_residual(a_out, b_out, w_out, j, x)


def _odd_layer(x, batch, seq, layer, j, norm_g, w_in, w_out, lb_raw, hg_norm_g):
    p = _norm_matmul(x, norm_g, w_in, j, 0, w_in.shape[-1], F32)
    o = _hgrn2(p, lb_raw, hg_norm_g, batch, seq, layer)
    return _matmul_residual(o, w_out, j, x, tm=2048)


def kernel(x, attn_norm_g, ffn_norm_g, final_norm_g, even_w_in, even_w_out, da_lambda, da_subln_g, rw_mu, rw_w0, rw_w_up, rw_a0, rw_a_up, rw_g_up, rw_k_k, rw_k_a, rw_r_k, rw_ln_w, rw_ln_b, odd_w_in, odd_w_out, hg_lower_bound, hg_norm_g, ffn_w_gate, ffn_w_up, ffn_w_down):
    batch, seq, d = x.shape
    depth = attn_norm_g.shape[0]
    xs = x.reshape(batch * seq, d)
    for layer in range(depth):
        j = layer // 2
        if layer % 2 == 0:
            xs = _even_layer(xs, batch, seq, layer, j, attn_norm_g[layer], even_w_in, even_w_out, da_lambda[j],
                             da_subln_g[j], rw_mu[j], rw_w0[j], rw_w_up[j], rw_a0[j], rw_a_up[j], rw_g_up[j],
                             rw_k_k[j], rw_k_a[j], rw_r_k[j], rw_ln_w[j], rw_ln_b[j])
        else:
            xs = _odd_layer(xs, batch, seq, layer, j, attn_norm_g[layer], odd_w_in, odd_w_out,
                            hg_lower_bound, hg_norm_g[j])
        xs = _ffn(xs, layer, ffn_norm_g[layer], ffn_w_gate, ffn_w_up, ffn_w_down)
    return _final_norm(xs, final_norm_g).reshape(batch, seq, d)
```

```python
import functools
import math

import jax
import jax.numpy as jnp
from jax import lax
from jax.experimental import pallas as pl
from jax.experimental.pallas import tpu as pltpu

D_MODEL = 2048
EPS = 1e-6

DA_WIDTH = D_MODEL // 2
DA_HEADS = 8
DA_HEAD_DIM = DA_WIDTH // DA_HEADS // 2
DA_V_DIM = 2 * DA_HEAD_DIM
ROPE_DIM = DA_HEAD_DIM // 4
ROPE_THETA = 500000.0

RW_WIDTH = D_MODEL // 2
RW_HEAD = 64
RW_HEADS = RW_WIDTH // RW_HEAD
DECAY_LORA = 64
AAA_LORA = 64
GATE_LORA = 32
RW_LORA = DECAY_LORA + AAA_LORA + GATE_LORA
RW_LORA_PAD = 256
RW_GN_EPS = 64e-5
RW_CHUNK = 64

HG_EXPAND = 128
HG_HEADS = D_MODEL // HG_EXPAND
HG_CHUNK = 64
HG_SUB = 8

LANES = 128
NEG_BIG = -1e30
LOG2_E = 1.0 / math.log(2.0)
VMEM_LIMIT = 56 * 1024 * 1024

F32 = jnp.float32
BF16 = jnp.bfloat16

_NN = (((1,), (0,)), ((), ()))
_NT = (((1,), (1,)), ((), ()))
_TN = (((0,), (0,)), ((), ()))


def _dot_bf16(a, b, dims=_NN):
    return lax.dot_general(a.astype(BF16), b.astype(BF16), dims, preferred_element_type=F32)


def _sigmoid(x):
    return 1.0 / (1.0 + jnp.exp(-x))


def _params(*sem):
    return pltpu.CompilerParams(dimension_semantics=sem, vmem_limit_bytes=VMEM_LIMIT)


def _rmsnorm_body(x_ref, g_ref, o_ref):
    x = x_ref[...]
    o_ref[...] = (x * lax.rsqrt(jnp.mean(x * x, axis=-1, keepdims=True) + EPS) * g_ref[...]).astype(o_ref.dtype)


def _rmsnorm(x2d, g, out_dtype, tm=512):
    t, d = x2d.shape
    tm = min(tm, t)
    return pl.pallas_call(
        _rmsnorm_body,
        grid=(t // tm,),
        in_specs=[pl.BlockSpec((tm, d), lambda i: (i, 0)), pl.BlockSpec((1, d), lambda i: (0, 0))],
        out_specs=pl.BlockSpec((tm, d), lambda i: (i, 0)),
        out_shape=jax.ShapeDtypeStruct((t, d), out_dtype),
        compiler_params=_params("parallel"),
        name="rmsnorm",
    )(x2d, g.reshape(1, d))


def _mm_body(a_ref, w_ref, o_ref):
    o_ref[...] = jnp.dot(a_ref[...], w_ref[...].astype(BF16), preferred_element_type=F32).astype(o_ref.dtype)


def _matmul(a, w, layer, col0, n, out_dtype, tm=1024, tn=512):
    m, k = a.shape
    tm, tn = min(tm, m), min(tn, n)
    c0 = col0 // tn
    return pl.pallas_call(
        _mm_body,
        grid=(m // tm, n // tn),
        in_specs=[pl.BlockSpec((tm, k), lambda i, j: (i, 0)),
                  pl.BlockSpec((None, k, tn), lambda i, j: (layer, 0, c0 + j))],
        out_specs=pl.BlockSpec((tm, tn), lambda i, j: (i, j)),
        out_shape=jax.ShapeDtypeStruct((m, n), out_dtype),
        compiler_params=_params("parallel", "arbitrary"),
        name="matmul",
    )(a, w)


def _mm_res_body(a_ref, w_ref, r_ref, o_ref):
    o_ref[...] = r_ref[...] + jnp.dot(a_ref[...], w_ref[...].astype(BF16), preferred_element_type=F32)


def _matmul_residual(a, w, layer, res, tm=1024, tn=256):
    m, k = a.shape
    n = w.shape[-1]
    tm, tn = min(tm, m), min(tn, n)
    return pl.pallas_call(
        _mm_res_body,
        grid=(m // tm, n // tn),
        in_specs=[
            pl.BlockSpec((tm, k), lambda i, j: (i, 0)),
            pl.BlockSpec((None, k, tn), lambda i, j: (layer, 0, j)),
            pl.BlockSpec((tm, tn), lambda i, j: (i, j)),
        ],
        out_specs=pl.BlockSpec((tm, tn), lambda i, j: (i, j)),
        out_shape=jax.ShapeDtypeStruct((m, n), F32),
        compiler_params=_params("parallel", "arbitrary"),
        name="matmul_residual",
    )(a, w, res)


def _mm2_res_body(a1_ref, a2_ref, w1_ref, w2_ref, r_ref, o_ref):
    acc = jnp.dot(a1_ref[...], w1_ref[...].astype(BF16), preferred_element_type=F32)
    acc = acc + jnp.dot(a2_ref[...], w2_ref[...].astype(BF16), preferred_element_type=F32)
    o_ref[...] = r_ref[...] + acc


def _matmul2_residual(a1, a2, w, layer, res, tm=2048, tn=256):
    m, k1 = a1.shape
    n = w.shape[-1]
    tm, tn = min(tm, m), min(tn, n)
    return pl.pallas_call(
        _mm2_res_body,
        grid=(m // tm, n // tn),
        in_specs=[
            pl.BlockSpec((tm, k1), lambda i, j: (i, 0)),
            pl.BlockSpec((tm, k1), lambda i, j: (i, 0)),
            pl.BlockSpec((None, k1, tn), lambda i, j: (layer, 0, j)),
            pl.BlockSpec((None, k1, tn), lambda i, j: (layer, 1, j)),
            pl.BlockSpec((tm, tn), lambda i, j: (i, j)),
        ],
        out_specs=pl.BlockSpec((tm, tn), lambda i, j: (i, j)),
        out_shape=jax.ShapeDtypeStruct((m, n), F32),
        compiler_params=_params("parallel", "arbitrary"),
        name="out_proj_residual",
    )(a1, a2, w, w, res)


def _ffn_up_body(h_ref, wg_ref, wu_ref, o_ref):
    h = h_ref[...]
    gate = jnp.dot(h, wg_ref[...].astype(BF16), preferred_element_type=F32)
    up = jnp.dot(h, wu_ref[...].astype(BF16), preferred_element_type=F32)
    o_ref[...] = (gate * _sigmoid(gate) * up).astype(o_ref.dtype)


def _ffn_up(h, wg, wu, layer, tm=2048, tn=512):
    m, k = h.shape
    n = wg.shape[-1]
    tm, tn = min(tm, m), min(tn, n)
    wspec = pl.BlockSpec((None, k, tn), lambda i, j: (layer, 0, j))
    return pl.pallas_call(
        _ffn_up_body,
        grid=(m // tm, n // tn),
        in_specs=[pl.BlockSpec((tm, k), lambda i, j: (i, 0)), wspec, wspec],
        out_specs=pl.BlockSpec((tm, tn), lambda i, j: (i, j)),
        out_shape=jax.ShapeDtypeStruct((m, n), BF16),
        compiler_params=_params("parallel", "arbitrary"),
        name="ffn_up",
    )(h, wg, wu)


def _qk_rope_body(h_ref, w_ref, cos_ref, sin_lo_ref, sin_hi_ref, o_ref, *, n_q_tiles, tn, q_scale, row_parts):
    w = w_ref[...].astype(BF16)
    scale = jnp.where(pl.program_id(1) < n_q_tiles, q_scale, 1.0).astype(F32)
    half = ROPE_DIM // 2
    tm = h_ref.shape[0]
    rows_per_part = tm // row_parts
    accs = [jnp.dot(h_ref[r0:r0 + rows_per_part, :], w, preferred_element_type=F32)
            for r0 in range(0, tm, rows_per_part)]
    for p, acc in enumerate(accs):
        rows = slice(p * rows_per_part, (p + 1) * rows_per_part)
        cos = cos_ref[rows, :]
        sin_lo = sin_lo_ref[rows, :]
        sin_hi = sin_hi_ref[rows, :]
        for g in range(tn // LANES):
            x = acc[:, g * LANES:(g + 1) * LANES]
            y = x * cos + pltpu.roll(x, half, 1) * sin_hi + pltpu.roll(x, LANES - half, 1) * sin_lo
            o_ref[rows, g * LANES:(g + 1) * LANES] = (y * scale).astype(o_ref.dtype)


def _rope_tables(seq):
    pos = jnp.arange(seq, dtype=F32)
    inv_freq = 1.0 / (ROPE_THETA ** (jnp.arange(0, ROPE_DIM, 2, dtype=F32) / ROPE_DIM))
    ang = pos[:, None] * inv_freq[None, :]
    cos, sin = jnp.cos(ang), jnp.sin(ang)
    half = ROPE_DIM // 2
    pad = jnp.zeros((seq, DA_HEAD_DIM - ROPE_DIM), F32)
    zer = jnp.zeros((seq, half), F32)
    cos_c = jnp.concatenate([cos, cos, pad + 1.0], axis=1)
    sin_lo = jnp.concatenate([-sin, zer, pad], axis=1)
    sin_hi = jnp.concatenate([zer, sin, pad], axis=1)
    rep = lambda t: jnp.concatenate([t, t], axis=1)
    return rep(cos_c), rep(sin_lo), rep(sin_hi)


def _qk_proj(h, w_in, layer, seq, tm=1024, tn=512):
    m, k = h.shape
    n = 2 * DA_WIDTH
    tm, tn = min(tm, seq), min(tn, n)
    cos, sin_lo, sin_hi = _rope_tables(seq)
    s_tiles = seq // tm
    body = functools.partial(_qk_rope_body, n_q_tiles=DA_WIDTH // tn, tn=tn, q_scale=DA_HEAD_DIM ** -0.5 * LOG2_E,
                             row_parts=8)
    tab = pl.BlockSpec((tm, LANES), lambda i, j: (i % s_tiles, 0))
    return pl.pallas_call(
        body,
        grid=(m // tm, n // tn),
        in_specs=[pl.BlockSpec((tm, k), lambda i, j: (i, 0)),
                  pl.BlockSpec((None, k, tn), lambda i, j: (layer, 0, j)), tab, tab, tab],
        out_specs=pl.BlockSpec((tm, tn), lambda i, j: (i, j)),
        out_shape=jax.ShapeDtypeStruct((m, n), BF16),
        compiler_params=_params("parallel", "arbitrary"),
        name="qk_proj_rope",
    )(h, w_in, cos, sin_lo, sin_hi)


def _attn_body(lam_ref, g_ref, q_ref, k_ref, v_ref, o_ref, *, tq, tk, lam_init, group, unroll):
    qi = pl.program_id(2)
    lane = lax.broadcasted_iota(jnp.int32, (1, LANES), 1)
    chains = []
    for h in range(group):
        ln = slice(h * LANES, (h + 1) * LANES)
        q = q_ref[:, ln]
        zero = jnp.zeros_like(q)
        chains.append((ln, jnp.where(lane < DA_HEAD_DIM, q, zero)))
        chains.append((ln, jnp.where(lane >= DA_HEAD_DIM, q, zero)))

    def score(c, rows, keep=None):
        ln, q = chains[c]
        s = lax.dot_general(k_ref[rows, ln], q, _NT, preferred_element_type=F32)
        if keep is not None:
            s = jnp.where(keep, s, NEG_BIG)
        return s, jnp.max(s, axis=0, keepdims=True)

    def scores(rows, keep=None):
        return tuple(score(c, rows, keep) for c in range(len(chains)))

    def absorb(rows, s, carry, next_rows=None):
        n = len(chains)
        s_next = [None] * n
        if next_rows is not None:
            s_next[0] = score(0, next_rows)
        pending = []
        for c in range(n):
            if next_rows is not None and c + 1 < n:
                s_next[c + 1] = score(c + 1, next_rows)
            m = carry[c][0]
            m_new = jnp.maximum(m, s[c][1])
            p = jnp.exp2(s[c][0] - m_new)
            pv = lax.dot_general(v_ref[rows, chains[c][0]], p.astype(BF16), _TN, preferred_element_type=F32)
            pending.append((m_new, jnp.exp2(m - m_new), jnp.sum(p, axis=0, keepdims=True), pv))
        out = tuple((mn, alpha * l + ps, alpha * acc + pv)
                    for (_, l, acc), (mn, alpha, ps, pv) in zip(carry, pending))
        return tuple(s_next), out

    def block(j):
        return pl.ds(pl.multiple_of(j * tk, tk), tk)

    init = tuple((jnp.full((1, tq), NEG_BIG, F32), jnp.zeros((1, tq), F32), jnp.zeros((LANES, tq), F32))
                 for _ in chains)
    per_q = tq // tk
    n_full = qi * per_q

    def many_blocks(t, state):
        for u in range(unroll):
            j = unroll * t + u
            state = absorb(block(j), *state, next_rows=block(j + 1))
        return state
    s, carry = lax.fori_loop(0, n_full // unroll, many_blocks, (scores(block(0)), init))

    def one_block(j, state):
        return absorb(block(j), *state, next_rows=block(j + 1))
    s, carry = lax.fori_loop(n_full // unroll * unroll, n_full, one_block, (s, carry))

    for d in range(per_q):
        key = lax.broadcasted_iota(jnp.int32, (tk, tq), 0) + d * tk
        keep = key <= lax.broadcasted_iota(jnp.int32, (tk, tq), 1)
        if d == 0:
            s = tuple((jnp.where(keep, t, NEG_BIG), jnp.max(jnp.where(keep, t, NEG_BIG), axis=0, keepdims=True))
                      for t, _ in s)
        else:
            s = scores(block(n_full + d), keep)
        _, carry = absorb(block(n_full + d), s, carry)

    lf = lam_ref[...]
    lam = (jnp.exp(jnp.sum(lf[0:1] * lf[1:2], axis=-1, keepdims=True))
           - jnp.exp(jnp.sum(lf[2:3] * lf[3:4], axis=-1, keepdims=True)) + lam_init)
    for h in range(group):
        (ln, _), (_, l0, acc0), (_, l1, acc1) = chains[2 * h], carry[2 * h], carry[2 * h + 1]
        o = acc0 / l0 - lam * (acc1 / l1)
        o = o * lax.rsqrt(jnp.mean(o * o, axis=0, keepdims=True) + EPS)
        o_ref[:, ln] = (o.T * g_ref[...] * (1.0 - lam_init)).astype(o_ref.dtype)


def _diff_attention(qk, v, lam_vec, subln_g, batch, seq, lam_init, tq=256, tk=256, group=4, unroll=4):
    tq = min(tq, seq)
    tk = min(tk, tq)
    nq = seq // tq
    n_groups = DA_HEADS // group
    width = group * LANES
    body = functools.partial(_attn_body, tq=tq, tk=tk, lam_init=lam_init, group=group, unroll=unroll)
    return pl.pallas_call(
        body,
        grid=(batch, n_groups, nq),
        in_specs=[
            pl.BlockSpec((4, DA_HEAD_DIM), lambda b, h, i: (0, 0)),
            pl.BlockSpec((1, DA_V_DIM), lambda b, h, i: (0, 0)),
            pl.BlockSpec((tq, width), lambda b, h, i: (b * nq + i, h)),
            pl.BlockSpec((seq, width), lambda b, h, i: (b, n_groups + h)),
            pl.BlockSpec((seq, width), lambda b, h, i: (b, h)),
        ],
        out_specs=pl.BlockSpec((tq, width), lambda b, h, i: (b * nq + i, h)),
        out_shape=jax.ShapeDtypeStruct((batch * seq, DA_WIDTH), BF16),
        compiler_params=_params("parallel", "parallel", "arbitrary"),
        name="diff_attention",
    )(lam_vec, subln_g.reshape(1, DA_V_DIM), qk, qk, v)


def _rwkv_prep_body(x_ref, xprev_ref, l_ref, lprev_ref, mu_ref, mul_ref, w0_ref, a0_ref, kk_ref, ka_ref,
                    wup_ref, aup_ref, gup_ref,
                    r_out, k_out, v_out, lw_out, kk_out, a_out, g_out, *, ts):
    first = pl.program_id(1) == 0
    row = lax.broadcasted_iota(jnp.int32, (ts, 1), 0)

    def token_shift(cur_ref, prev_ref, mu):
        cur = cur_ref[0]
        last = jnp.where(first, 0.0, prev_ref[0][7:8, :])
        prev = jnp.where(row == 0, last, pltpu.roll(cur, 1, 0))
        return cur + (prev - cur) * mu

    xm = token_shift(x_ref, xprev_ref, mu_ref[...])
    xl = token_shift(l_ref, lprev_ref, mul_ref[...])
    r = xm[:, :RW_WIDTH]
    k = xm[:, RW_WIDTH:2 * RW_WIDTH]
    v = xm[:, 2 * RW_WIDTH:]
    z = -(w0_ref[...] + _dot_bf16(jnp.tanh(xl), wup_ref[...]))
    softplus = jnp.maximum(z, 0.0) + jnp.log(1.0 + jnp.exp(-jnp.abs(z)))
    w_log = -softplus - 0.5
    a = _sigmoid(a0_ref[...] + _dot_bf16(xl, aup_ref[...]))
    g = _dot_bf16(_sigmoid(xl), gup_ref[...])
    r_out[0] = r
    k_out[0] = k * (1.0 + (a - 1.0) * ka_ref[...])
    v_out[0] = v
    lw_out[0] = -jnp.exp(w_log)
    kk_out[0] = k * kk_ref[...]
    a_out[0] = a
    g_out[0] = g


def _rwkv_prep(prw, plora, mu, w0, w_up, a0, a_up, g_up, k_k, k_a, ts=256):
    b, s, w3 = prw.shape
    ts = min(ts, s)
    wl = plora.shape[-1]
    pad_rows = lambda m, lo: jnp.zeros((wl, RW_WIDTH), F32).at[lo:lo + m.shape[0]].set(m)
    wup = pad_rows(w_up, 0)
    aup = pad_rows(a_up, DECAY_LORA)
    gup = pad_rows(g_up, DECAY_LORA + AAA_LORA)
    mu_main = mu[:w3].reshape(1, w3)
    mu_lora = jnp.zeros((1, wl), F32).at[0, :RW_LORA].set(mu[w3:])
    row = lambda t: t.reshape(1, RW_WIDTH)
    blk = lambda w: pl.BlockSpec((1, ts, w), lambda bi, i: (bi, i, 0))
    prev = lambda w: pl.BlockSpec((1, 8, w), lambda bi, i: (bi, jnp.maximum(i * (ts // 8) - 1, 0), 0))
    full = lambda r, c: pl.BlockSpec((r, c), lambda bi, i: (0, 0))
    out_sds = jax.ShapeDtypeStruct((b, s, RW_WIDTH), F32)
    return pl.pallas_call(
        functools.partial(_rwkv_prep_body, ts=ts),
        grid=(b, s // ts),
        in_specs=[blk(w3), prev(w3), blk(wl), prev(wl), full(1, w3), full(1, wl),
                  full(1, RW_WIDTH), full(1, RW_WIDTH), full(1, RW_WIDTH), full(1, RW_WIDTH),
                  full(wl, RW_WIDTH), full(wl, RW_WIDTH), full(wl, RW_WIDTH)],
        out_specs=[blk(RW_WIDTH)] * 7,
        out_shape=[out_sds] * 7,
        compiler_params=_params("parallel", "parallel"),
        name="rwkv_prep",
    )(prw, prw, plora, plora, mu_main, mu_lora, row(w0), row(a0), row(k_k), row(k_a), wup, aup, gup)


def _cumsum_rows(ones_lower_bf16, x):
    hi = x.astype(BF16)
    rem = x - hi.astype(F32)
    mid = rem.astype(BF16)
    lo = (rem - mid.astype(F32)).astype(BF16)
    width = x.shape[1]
    out = jnp.dot(ones_lower_bf16, jnp.concatenate([hi, mid, lo], axis=1), preferred_element_type=F32)
    return out[:, :width] + out[:, width:2 * width] + out[:, 2 * width:]


def _rwkv_chunk_body(r_ref, k_ref, v_ref, lw_ref, kk_ref, a_ref, g_ref, rk_ref, lnw_ref, lnb_ref,
                     o_ref, st_ref, *, chunk, n_chunks, n_pairs, group):
    @pl.when(pl.program_id(1) == 0)
    def _():
        st_ref[...] = jnp.zeros_like(st_ref)

    c = chunk
    head = RW_HEAD
    lane = lax.broadcasted_iota(jnp.int32, (1, LANES), 1)
    head0 = lane < head
    row_c = lax.broadcasted_iota(jnp.int32, (c, LANES), 0)
    col_c = lax.broadcasted_iota(jnp.int32, (c, LANES), 1)
    col_in_head = jnp.where(col_c < head, col_c, col_c - head)
    strict = col_in_head < row_c
    lower = col_in_head <= row_c
    row_p = lax.broadcasted_iota(jnp.int32, (LANES, LANES), 0)
    col_p = lax.broadcasted_iota(jnp.int32, (LANES, LANES), 1)
    eye = jnp.where(row_p == col_p, 1.0, 0.0).astype(F32)
    same_head = (row_p < head) == (col_p < head)
    ones_lower = jnp.where(lax.broadcasted_iota(jnp.int32, (c, c), 1) <= lax.broadcasted_iota(jnp.int32, (c, c), 0),
                           1.0, 0.0).astype(BF16)
    zeros_c = jnp.zeros((c, LANES), BF16)
    n_doublings = int(math.log2(c)) - 1

    def only0(x):
        return jnp.where(head0, x, jnp.zeros_like(x))

    def only1(x):
        return jnp.where(head0, jnp.zeros_like(x), x)

    def head_sum(x):
        s0 = jnp.sum(only0(x), axis=-1, keepdims=True)
        s1 = jnp.sum(only1(x), axis=-1, keepdims=True)
        return jnp.where(head0, s0, s1)

    def each(fn, *lists):
        return [fn(*args) for args in zip(*lists)]

    def chunk_group(rows, pairs):
        lanes = [slice(p * LANES, (p + 1) * LANES) for p in pairs]
        load = lambda ref: [ref[0, rows, ln] for ln in lanes]
        r, k, v, lw, kkr, ag = load(r_ref), load(k_ref), load(v_ref), load(lw_ref), load(kk_ref), load(a_ref)
        kk = each(lambda t: t / jnp.maximum(jnp.sqrt(head_sum(t * t)), 1e-12), kkr)
        bm = each(lambda a, b: a * b, kk, ag)
        cw = each(lambda t: _cumsum_rows(ones_lower, t), lw)
        w_inv = each(lambda t: jnp.exp(-t), cw)
        b_t = each(lambda a, b: (a * b).astype(BF16), bm, w_inv)
        k_t = each(lambda a, b: (a * b).astype(BF16), k, w_inv)
        lhs = each(lambda kk_, cw_, lw_, r_: jnp.concatenate([-kk_ * jnp.exp(cw_ - lw_), r_ * jnp.exp(cw_)],
                                                             axis=0).astype(BF16), kk, cw, lw, r)
        g0 = each(lambda l, b, k_: lax.dot_general(only0(l), jnp.concatenate([b, k_], axis=0), _NT,
                                                   preferred_element_type=F32), lhs, b_t, k_t)
        g1 = each(lambda l, b, k_: lax.dot_general(only1(l), jnp.concatenate([k_, b], axis=0), _NT,
                                                   preferred_element_type=F32), lhs, b_t, k_t)
        top0 = each(lambda g_: jnp.where(strict, g_[:c], 0.0), g0)
        top1 = each(lambda g_: jnp.where(strict, g_[:c], 0.0), g1)
        bot = each(lambda a, b: jnp.concatenate([jnp.where(lower, a[c:], 0.0), jnp.where(lower, b[c:], 0.0)],
                                                axis=1).astype(BF16), g0, g1)
        power = each(lambda a, b: jnp.concatenate([only0(a), only1(b)], axis=0), top0, top1)
        inv = each(lambda n: eye + n, power)
        for _ in range(n_doublings):
            power = each(lambda t: jnp.dot(t.astype(BF16), t.astype(BF16), preferred_element_type=F32), power)
            inv = each(lambda t, pw: t + jnp.dot(t.astype(BF16), pw.astype(BF16), preferred_element_type=F32),
                       inv, power)
        st = [st_ref[p] for p in pairs]
        from_state = each(lambda l, s_: lax.dot_general(l, s_.astype(BF16), _NT, preferred_element_type=F32),
                          lhs, st)
        vb = each(lambda t: t.astype(BF16), v)
        x = each(lambda fs, a, b, v_: fs[:c] + jnp.dot(
            jnp.concatenate([a, b], axis=1).astype(BF16),
            jnp.concatenate([zeros_c, only0(v_), only1(v_), zeros_c], axis=0), preferred_element_type=F32),
            from_state, top0, top1, vb)
        tu = each(lambda t, x_: jnp.dot(t.astype(BF16), jnp.concatenate([only0(x_), only1(x_)], axis=0).astype(BF16),
                                        preferred_element_type=F32), inv, x)
        ub = each(lambda t: (t[:c] + t[c:]).astype(BF16), tu)
        y = each(lambda fs, b, u_, v_: fs[c:] + jnp.dot(
            b, jnp.concatenate([only0(u_), only0(v_), only1(v_), only1(u_)], axis=0), preferred_element_type=F32),
            from_state, bot, ub, vb)
        w_end = each(lambda t: t[c - 1:c], cw)
        tail = each(lambda e, t: jnp.exp(e - t), w_end, cw)
        upd = each(lambda u_, v_, bm_, k_, tl: lax.dot_general(
            jnp.concatenate([u_, v_], axis=0),
            jnp.concatenate([(bm_ * tl).astype(BF16), (k_ * tl).astype(BF16)], axis=0), _TN,
            preferred_element_type=F32), ub, vb, bm, k, tail)
        for p, s_, e, up in zip(pairs, st, w_end, upd):
            st_ref[p] = s_ * jnp.exp(e) + jnp.where(same_head, up, 0.0)
        for ln, y_, r_, k_, v_ in zip(lanes, y, r, k, v):
            yc = y_ - head_sum(y_) * (1.0 / head)
            var = head_sum(yc * yc) * (1.0 / head)
            yn = yc * lax.rsqrt(var + RW_GN_EPS) * lnw_ref[:, ln] + lnb_ref[:, ln]
            bonus = head_sum(r_ * k_ * rk_ref[:, ln]) * v_
            o_ref[rows, ln] = ((yn + bonus) * g_ref[0, rows, ln]).astype(o_ref.dtype)

    for ci in range(n_chunks):
        for p0 in range(0, n_pairs, group):
            chunk_group(slice(ci * c, (ci + 1) * c), list(range(p0, min(p0 + group, n_pairs))))


def _rwkv_chunked(r, k, v, lw, kk, a, g, r_k, ln_w, ln_b, chunk=RW_CHUNK, block=RW_CHUNK, group=8):
    b, s, w = r.shape
    chunk = min(chunk, s)
    block = min(block, s)
    nb = s // block
    n_pairs = w // LANES
    blk = pl.BlockSpec((1, block, w), lambda bi, i: (bi, i, 0))
    par = pl.BlockSpec((1, w), lambda bi, i: (0, 0))
    return pl.pallas_call(
        functools.partial(_rwkv_chunk_body, chunk=chunk, n_chunks=block // chunk, n_pairs=n_pairs, group=group),
        grid=(b, nb),
        in_specs=[blk] * 7 + [par] * 3,
        out_specs=pl.BlockSpec((block, w), lambda bi, i: (bi * nb + i, 0)),
        out_shape=jax.ShapeDtypeStruct((b * s, w), BF16),
        scratch_shapes=[pltpu.VMEM((n_pairs, LANES, LANES), F32)],
        compiler_params=_params("parallel", "arbitrary"),
        name="rwkv_chunk",
    )(r, k, v, lw, kk, a, g, r_k.reshape(1, w), ln_w.reshape(1, w), ln_b.reshape(1, w))


def _hgrn_body(lbraw_ref, ng_ref, q_ref, f_ref, i_ref, g_ref, o_ref, st_ref, *, block, layer, group):
    @pl.when(pl.program_id(2) == 0)
    def _():
        st_ref[...] = jnp.zeros_like(st_ref)

    lbraw = lbraw_ref[...]
    e = jnp.exp(lbraw - jnp.max(lbraw, axis=0, keepdims=True))
    soft = e / jnp.sum(e, axis=0, keepdims=True)
    lb_all = jnp.sum(soft[:layer + 1], axis=0, keepdims=True) - soft[0:1]
    ng = ng_ref[...]

    c, sub = HG_CHUNK, HG_SUB
    n_sub = c // sub
    row = lax.broadcasted_iota(jnp.int32, (c, c), 0)
    col = lax.broadcasted_iota(jnp.int32, (c, c), 1)
    ones_lower = jnp.where(col <= row, 1.0, 0.0).astype(BF16)
    ones_sq = jnp.ones((LANES, LANES), BF16)
    ri = lax.broadcasted_iota(jnp.int32, (c, 1), 0)
    rs = lax.broadcasted_iota(jnp.int32, (sub, 1), 0)
    zeros_sub = jnp.zeros((sub, LANES), F32)
    off_row = lax.broadcasted_iota(jnp.int32, (c, (n_sub - 1) * c), 0) // sub
    off_col = lax.broadcasted_iota(jnp.int32, (c, (n_sub - 1) * c), 1) // c
    off_keep = off_row == off_col + 1
    lanes = [slice(h * LANES, (h + 1) * LANES) for h in range(group)]
    lb = [lb_all[:, ln] for ln in lanes]

    def each(fn, *lists):
        return [fn(*args) for args in zip(*lists)]

    def diag_pieces(q, k, b):
        pieces = []
        for i in range(n_sub):
            sl = slice(sub * i, sub * (i + 1))
            qi, ki, bi = q[sl], k[sl], b[sl] * LOG2_E
            for s in range(sub):
                decay = jnp.exp2(jnp.where(rs >= s, bi - bi[s:s + 1], NEG_BIG))
                pieces.append(qi * decay * ki[s:s + 1])
        return jnp.concatenate(pieces, axis=0)

    def diag_apply(sums, iv):
        parts = []
        for i in range(n_sub):
            od = zeros_sub
            for s in range(sub):
                r0 = (i * sub + s) * sub
                od = od + sums[r0:r0 + sub] * iv[sub * i + s:sub * i + s + 1]
            parts.append(od)
        return jnp.concatenate(parts, axis=0)

    def chunk_step(ci, carry):
        rows = pl.ds(pl.multiple_of(ci * c, c), c)
        q = [q_ref[rows, ln] for ln in lanes]
        iv = [i_ref[rows, ln] for ln in lanes]
        fg = each(lambda ln, lb_: lb_ + (1.0 - lb_) * _sigmoid(f_ref[rows, ln]), lanes, lb)
        k = each(lambda t: 1.0 - t, fg)
        b = each(lambda t: _cumsum_rows(ones_lower, jnp.log(t)), fg)
        st = [st_ref[h] for h in range(group)]
        o_inter = each(lambda q_, b_, s_: _dot_bf16(q_ * jnp.exp(b_), s_, _NT), q, b, st)

        def off_scores(q_, k_, b_):
            brefs = [b_[sub * i - 1:sub * i] for i in range(1, n_sub)]
            bref_rows = jnp.concatenate([zeros_sub] + [jnp.broadcast_to(t, (sub, LANES)) for t in brefs], axis=0)
            qh = q_ * jnp.exp(jnp.where(ri >= sub, b_ - bref_rows, NEG_BIG))
            kh = jnp.concatenate([k_ * jnp.exp(jnp.where(ri < sub * (i + 1), t - b_, NEG_BIG))
                                  for i, t in enumerate(brefs)], axis=0)
            return _dot_bf16(qh, kh, _NT)
        scores = each(off_scores, q, k, b)

        stacked = each(diag_pieces, q, k, b)
        sums = each(lambda t: jnp.dot(t.astype(BF16), ones_sq, preferred_element_type=F32), stacked)
        ivb = each(lambda t: t.astype(BF16), iv)
        o_off = each(lambda sc, v_: jnp.dot(jnp.where(off_keep, sc, 0.0).astype(BF16),
                                            jnp.concatenate([v_] * (n_sub - 1), axis=0),
                                            preferred_element_type=F32), scores, ivb)
        b_end = each(lambda t: t[c - 1:c], b)
        upd = each(lambda v_, k_, b_, e_: lax.dot_general(v_, (k_ * jnp.exp(e_ - b_)).astype(BF16), _TN,
                                                          preferred_element_type=F32), ivb, k, b, b_end)
        for h in range(group):
            st_ref[h] = st[h] * jnp.exp(b_end[h]) + upd[h]
            o = o_inter[h] + o_off[h] + diag_apply(sums[h], iv[h])
            o = o * lax.rsqrt(jnp.mean(o * o, axis=-1, keepdims=True) + EPS) * ng
            gg = g_ref[rows, lanes[h]]
            o_ref[rows, lanes[h]] = (o * (gg * _sigmoid(gg))).astype(o_ref.dtype)
        return carry

    lax.fori_loop(0, block // c, chunk_step, 0, unroll=8)


def _hgrn2(p, lb_raw, norm_g, batch, seq, layer, block=512, group=4):
    block = min(block, seq)
    nb = seq // block
    depth = lb_raw.shape[0]
    n_groups = HG_HEADS // group
    width = group * LANES
    col = lambda part: pl.BlockSpec((block, width), lambda b, h, i: (b * nb + i, part * n_groups + h))
    return pl.pallas_call(
        functools.partial(_hgrn_body, block=block, layer=layer, group=group),
        grid=(batch, n_groups, nb),
        in_specs=[pl.BlockSpec((depth, width), lambda b, h, i: (0, h)),
                  pl.BlockSpec((1, LANES), lambda b, h, i: (0, 0)),
                  col(0), col(1), col(2), col(3)],
        out_specs=pl.BlockSpec((block, width), lambda b, h, i: (b * nb + i, h)),
        out_shape=jax.ShapeDtypeStruct((batch * seq, D_MODEL), BF16),
        scratch_shapes=[pltpu.VMEM((group, LANES, LANES), F32)],
        compiler_params=_params("parallel", "parallel", "arbitrary"),
        name="hgrn2",
    )(lb_raw, norm_g.reshape(1, LANES), p, p, p, p)


def _ffn(x, layer, norm_g, w_gate, w_up, w_down):
    hidden = _ffn_up(_rmsnorm(x, norm_g, BF16), w_gate, w_up, layer)
    return _matmul_residual(hidden, w_down, layer, x)


def _even_layer(x, batch, seq, layer, j, norm_g, w_in, w_out, lam_vec, subln_g, mu, w0, w_up, a0, a_up, g_up,
                k_k, k_a, r_k, ln_w, ln_b):
    h = _rmsnorm(x, norm_g, BF16)
    qk = _qk_proj(h, w_in, j, seq)
    v = _matmul(h, w_in, j, 2 * DA_WIDTH, DA_WIDTH, BF16)
    lam_init = 0.8 - 0.6 * math.exp(-0.3 * layer)
    a_out = _diff_attention(qk, v, lam_vec, subln_g, batch, seq, lam_init)

    rw0 = 3 * DA_WIDTH
    prw = _matmul(h, w_in, j, rw0, 3 * RW_WIDTH, F32)
    w_lora = jnp.pad(w_in[j, :, rw0 + 3 * RW_WIDTH:], ((0, 0), (0, RW_LORA_PAD - RW_LORA)))
    plora = _matmul(h, w_lora[None], 0, 0, RW_LORA_PAD, F32)
    tok = _rwkv_prep(prw.reshape(batch, seq, -1), plora.reshape(batch, seq, -1), mu, w0, w_up, a0, a_up, g_up,
                     k_k, k_a)
    b_out = _rwkv_chunked(*tok, r_k, ln_w, ln_b)
    return _matmul2_residual(a_out, b_out, w_out, j, x)


def _odd_layer(x, batch, seq, layer, j, norm_g, w_in, w_out, lb_raw, hg_norm_g):
    p = _matmul(_rmsnorm(x, norm_g, BF16), w_in, j, 0, w_in.shape[-1], F32, tm=2048)
    o = _hgrn2(p, lb_raw, hg_norm_g, batch, seq, layer)
    return _matmul_residual(o, w_out, j, x, tm=2048)


def kernel(x, attn_norm_g, ffn_norm_g, final_norm_g, even_w_in, even_w_out, da_lambda, da_subln_g, rw_mu, rw_w0, rw_w_up, rw_a0, rw_a_up, rw_g_up, rw_k_k, rw_k_a, rw_r_k, rw_ln_w, rw_ln_b, odd_w_in, odd_w_out, hg_lower_bound, hg_norm_g, ffn_w_gate, ffn_w_up, ffn_w_down):
    batch, seq, d = x.shape
    depth = attn_norm_g.shape[0]
    xs = x.reshape(batch * seq, d)
    w_down = ffn_w_down.astype(BF16)
    for layer in range(depth):
        j = layer // 2
        if layer % 2 == 0:
            xs = _even_layer(xs, batch, seq, layer, j, attn_norm_g[layer], even_w_in, even_w_out, da_lambda[j],
                             da_subln_g[j], rw_mu[j], rw_w0[j], rw_w_up[j], rw_a0[j], rw_a_up[j], rw_g_up[j],
                             rw_k_k[j], rw_k_a[j], rw_r_k[j], rw_ln_w[j], rw_ln_b[j])
        else:
            xs = _odd_layer(xs, batch, seq, layer, j, attn_norm_g[layer], odd_w_in, odd_w_out,
                            hg_lower_bound, hg_norm_g[j])
        xs = _ffn(xs, layer, ffn_norm_g[layer], ffn_w_gate, ffn_w_up, w_down)
    return _rmsnorm(xs, final_norm_g, F32).reshape(batch, seq, d)
```

```python
import functools
import math

import jax
import jax.numpy as jnp
from jax import lax
from jax.experimental import pallas as pl
from jax.experimental.pallas import tpu as pltpu

D_MODEL = 2048
EPS = 1e-6

DA_WIDTH = D_MODEL // 2
DA_HEADS = 8
DA_HEAD_DIM = DA_WIDTH // DA_HEADS // 2
DA_V_DIM = 2 * DA_HEAD_DIM
ROPE_DIM = DA_HEAD_DIM // 4
ROPE_THETA = 500000.0

RW_WIDTH = D_MODEL // 2
RW_HEAD = 64
RW_HEADS = RW_WIDTH // RW_HEAD
DECAY_LORA = 64
AAA_LORA = 64
GATE_LORA = 32
RW_LORA = DECAY_LORA + AAA_LORA + GATE_LORA
RW_LORA_PAD = 256
RW_GN_EPS = 64e-5
RW_CHUNK = 64

HG_EXPAND = 128
HG_HEADS = D_MODEL // HG_EXPAND
HG_CHUNK = 64
HG_SUB = 8

LANES = 128
NEG_BIG = -1e30
LOG2_E = 1.0 / math.log(2.0)
VMEM_LIMIT = 56 * 1024 * 1024

F32 = jnp.float32
BF16 = jnp.bfloat16

_NN = (((1,), (0,)), ((), ()))
_NT = (((1,), (1,)), ((), ()))
_TN = (((0,), (0,)), ((), ()))


def _dot_bf16(a, b, dims=_NN):
    return lax.dot_general(a.astype(BF16), b.astype(BF16), dims, preferred_element_type=F32)


def _sigmoid(x):
    return 1.0 / (1.0 + jnp.exp(-x))


def _params(*sem):
    return pltpu.CompilerParams(dimension_semantics=sem, vmem_limit_bytes=VMEM_LIMIT)


def _rmsnorm_body(x_ref, g_ref, o_ref):
    x = x_ref[...]
    o_ref[...] = (x * lax.rsqrt(jnp.mean(x * x, axis=-1, keepdims=True) + EPS) * g_ref[...]).astype(o_ref.dtype)


def _rmsnorm(x2d, g, out_dtype, tm=512):
    t, d = x2d.shape
    tm = min(tm, t)
    return pl.pallas_call(
        _rmsnorm_body,
        grid=(t // tm,),
        in_specs=[pl.BlockSpec((tm, d), lambda i: (i, 0)), pl.BlockSpec((1, d), lambda i: (0, 0))],
        out_specs=pl.BlockSpec((tm, d), lambda i: (i, 0)),
        out_shape=jax.ShapeDtypeStruct((t, d), out_dtype),
        compiler_params=_params("parallel"),
        name="rmsnorm",
    )(x2d, g.reshape(1, d))


def _mm_body(a_ref, w_ref, o_ref):
    o_ref[...] = jnp.dot(a_ref[...], w_ref[...].astype(BF16), preferred_element_type=F32).astype(o_ref.dtype)


def _matmul(a, w, layer, col0, n, out_dtype, tm=1024, tn=512):
    m, k = a.shape
    tm, tn = min(tm, m), min(tn, n)
    c0 = col0 // tn
    return pl.pallas_call(
        _mm_body,
        grid=(m // tm, n // tn),
        in_specs=[pl.BlockSpec((tm, k), lambda i, j: (i, 0)),
                  pl.BlockSpec((None, k, tn), lambda i, j: (layer, 0, c0 + j))],
        out_specs=pl.BlockSpec((tm, tn), lambda i, j: (i, j)),
        out_shape=jax.ShapeDtypeStruct((m, n), out_dtype),
        compiler_params=_params("parallel", "arbitrary"),
        name="matmul",
    )(a, w)


def _norm_rows(x, g):
    return x * lax.rsqrt(jnp.mean(x * x, axis=-1, keepdims=True) + EPS) * g


def _res_norm_body(*refs, n_lhs, emit_x):
    a_refs, w_refs = refs[:n_lhs], refs[n_lhs:2 * n_lhs]
    r_ref, g_ref = refs[2 * n_lhs:2 * n_lhs + 2]
    outs = refs[2 * n_lhs + 2:]
    x = r_ref[...]
    for a_ref, w_ref in zip(a_refs, w_refs):
        x = x + jnp.dot(a_ref[...], w_ref[...], preferred_element_type=F32)
    if emit_x:
        outs[0][...] = x
    outs[-1][...] = _norm_rows(x, g_ref[...]).astype(outs[-1].dtype)


def _matmul_residual_norm(lhs, w, layer, res, g, norm_dtype, emit_x=True, tm=512):
    m, n = res.shape
    k = lhs[0].shape[1]
    tm = min(tm, m)
    row = lambda width: pl.BlockSpec((tm, width), lambda i: (i, 0))
    w_specs = [pl.BlockSpec((None, k, n), lambda i, p=p: (layer, p, 0), pipeline_mode=pl.Buffered(1))
               for p in range(len(lhs))]
    out_specs = ([row(n)] if emit_x else []) + [row(n)]
    out_shape = ([jax.ShapeDtypeStruct((m, n), F32)] if emit_x else []) + [jax.ShapeDtypeStruct((m, n), norm_dtype)]
    outs = pl.pallas_call(
        functools.partial(_res_norm_body, n_lhs=len(lhs), emit_x=emit_x),
        grid=(m // tm,),
        in_specs=[row(k)] * len(lhs) + w_specs + [row(n), pl.BlockSpec((1, n), lambda i: (0, 0))],
        out_specs=out_specs,
        out_shape=out_shape,
        compiler_params=_params("parallel"),
        name="matmul_residual_norm",
    )(*lhs, *([w] * len(lhs)), res, g.reshape(1, n))
    return outs if emit_x else (None, outs[0])


def _ffn_up_body(h_ref, wg_ref, wu_ref, o_ref):
    h = h_ref[...]
    gate = jnp.dot(h, wg_ref[...].astype(BF16), preferred_element_type=F32)
    up = jnp.dot(h, wu_ref[...].astype(BF16), preferred_element_type=F32)
    o_ref[...] = (gate * _sigmoid(gate) * up).astype(o_ref.dtype)


def _ffn_up(h, wg, wu, layer, tm=2048, tn=512):
    m, k = h.shape
    n = wg.shape[-1]
    tm, tn = min(tm, m), min(tn, n)
    wspec = pl.BlockSpec((None, k, tn), lambda i, j: (layer, 0, j))
    return pl.pallas_call(
        _ffn_up_body,
        grid=(m // tm, n // tn),
        in_specs=[pl.BlockSpec((tm, k), lambda i, j: (i, 0)), wspec, wspec],
        out_specs=pl.BlockSpec((tm, tn), lambda i, j: (i, j)),
        out_shape=jax.ShapeDtypeStruct((m, n), BF16),
        compiler_params=_params("parallel", "arbitrary"),
        name="ffn_up",
    )(h, wg, wu)


def _qk_rope_body(h_ref, w_ref, cos_ref, sin_lo_ref, sin_hi_ref, o_ref, *, n_q_tiles, tn, q_scale, row_parts):
    w = w_ref[...].astype(BF16)
    scale = jnp.where(pl.program_id(1) < n_q_tiles, q_scale, 1.0).astype(F32)
    half = ROPE_DIM // 2
    tm = h_ref.shape[0]
    rows_per_part = tm // row_parts
    accs = [jnp.dot(h_ref[r0:r0 + rows_per_part, :], w, preferred_element_type=F32)
            for r0 in range(0, tm, rows_per_part)]
    for p, acc in enumerate(accs):
        rows = slice(p * rows_per_part, (p + 1) * rows_per_part)
        cos = cos_ref[rows, :]
        sin_lo = sin_lo_ref[rows, :]
        sin_hi = sin_hi_ref[rows, :]
        for g in range(tn // LANES):
            x = acc[:, g * LANES:(g + 1) * LANES]
            y = x * cos + pltpu.roll(x, half, 1) * sin_hi + pltpu.roll(x, LANES - half, 1) * sin_lo
            o_ref[rows, g * LANES:(g + 1) * LANES] = (y * scale).astype(o_ref.dtype)


def _rope_tables(seq):
    pos = jnp.arange(seq, dtype=F32)
    inv_freq = 1.0 / (ROPE_THETA ** (jnp.arange(0, ROPE_DIM, 2, dtype=F32) / ROPE_DIM))
    ang = pos[:, None] * inv_freq[None, :]
    cos, sin = jnp.cos(ang), jnp.sin(ang)
    half = ROPE_DIM // 2
    pad = jnp.zeros((seq, DA_HEAD_DIM - ROPE_DIM), F32)
    zer = jnp.zeros((seq, half), F32)
    cos_c = jnp.concatenate([cos, cos, pad + 1.0], axis=1)
    sin_lo = jnp.concatenate([-sin, zer, pad], axis=1)
    sin_hi = jnp.concatenate([zer, sin, pad], axis=1)
    rep = lambda t: jnp.concatenate([t, t], axis=1)
    return rep(cos_c), rep(sin_lo), rep(sin_hi)


def _qk_proj(h, w_in, layer, seq, tm=1024, tn=512):
    m, k = h.shape
    n = 2 * DA_WIDTH
    tm, tn = min(tm, seq), min(tn, n)
    cos, sin_lo, sin_hi = _rope_tables(seq)
    s_tiles = seq // tm
    body = functools.partial(_qk_rope_body, n_q_tiles=DA_WIDTH // tn, tn=tn, q_scale=DA_HEAD_DIM ** -0.5 * LOG2_E,
                             row_parts=8)
    tab = pl.BlockSpec((tm, LANES), lambda i, j: (i % s_tiles, 0))
    return pl.pallas_call(
        body,
        grid=(m // tm, n // tn),
        in_specs=[pl.BlockSpec((tm, k), lambda i, j: (i, 0)),
                  pl.BlockSpec((None, k, tn), lambda i, j: (layer, 0, j)), tab, tab, tab],
        out_specs=pl.BlockSpec((tm, tn), lambda i, j: (i, j)),
        out_shape=jax.ShapeDtypeStruct((m, n), BF16),
        compiler_params=_params("parallel", "arbitrary"),
        name="qk_proj_rope",
    )(h, w_in, cos, sin_lo, sin_hi)


def _attn_body(lam_ref, g_ref, q_ref, k_ref, v_ref, o_ref, *, tq, tk, lam_init, group, unroll):
    qi = pl.program_id(2)
    lane = lax.broadcasted_iota(jnp.int32, (1, LANES), 1)
    chains = []
    for h in range(group):
        ln = slice(h * LANES, (h + 1) * LANES)
        q = q_ref[:, ln]
        zero = jnp.zeros_like(q)
        chains.append((ln, jnp.where(lane < DA_HEAD_DIM, q, zero)))
        chains.append((ln, jnp.where(lane >= DA_HEAD_DIM, q, zero)))

    def score(c, rows, keep=None):
        ln, q = chains[c]
        s = lax.dot_general(k_ref[rows, ln], q, _NT, preferred_element_type=F32)
        if keep is not None:
            s = jnp.where(keep, s, NEG_BIG)
        return s, jnp.max(s, axis=0, keepdims=True)

    def scores(rows, keep=None):
        return tuple(score(c, rows, keep) for c in range(len(chains)))

    def absorb(rows, s, carry, next_rows=None):
        n = len(chains)
        s_next = [None] * n
        if next_rows is not None:
            s_next[0] = score(0, next_rows)
        pending = []
        for c in range(n):
            if next_rows is not None and c + 1 < n:
                s_next[c + 1] = score(c + 1, next_rows)
            m = carry[c][0]
            m_new = jnp.maximum(m, s[c][1])
            p = jnp.exp2(s[c][0] - m_new)
            pv = lax.dot_general(v_ref[rows, chains[c][0]], p.astype(BF16), _TN, preferred_element_type=F32)
            pending.append((m_new, jnp.exp2(m - m_new), jnp.sum(p, axis=0, keepdims=True), pv))
        out = tuple((mn, alpha * l + ps, alpha * acc + pv)
                    for (_, l, acc), (mn, alpha, ps, pv) in zip(carry, pending))
        return tuple(s_next), out

    def block(j):
        return pl.ds(pl.multiple_of(j * tk, tk), tk)

    init = tuple((jnp.full((1, tq), NEG_BIG, F32), jnp.zeros((1, tq), F32), jnp.zeros((LANES, tq), F32))
                 for _ in chains)
    per_q = tq // tk
    n_full = qi * per_q

    def many_blocks(t, state):
        for u in range(unroll):
            j = unroll * t + u
            state = absorb(block(j), *state, next_rows=block(j + 1))
        return state
    s, carry = lax.fori_loop(0, n_full // unroll, many_blocks, (scores(block(0)), init))

    def one_block(j, state):
        return absorb(block(j), *state, next_rows=block(j + 1))
    s, carry = lax.fori_loop(n_full // unroll * unroll, n_full, one_block, (s, carry))

    for d in range(per_q):
        key = lax.broadcasted_iota(jnp.int32, (tk, tq), 0) + d * tk
        keep = key <= lax.broadcasted_iota(jnp.int32, (tk, tq), 1)
        if d == 0:
            s = tuple((jnp.where(keep, t, NEG_BIG), jnp.max(jnp.where(keep, t, NEG_BIG), axis=0, keepdims=True))
                      for t, _ in s)
        else:
            s = scores(block(n_full + d), keep)
        _, carry = absorb(block(n_full + d), s, carry)

    lf = lam_ref[...]
    lam = (jnp.exp(jnp.sum(lf[0:1] * lf[1:2], axis=-1, keepdims=True))
           - jnp.exp(jnp.sum(lf[2:3] * lf[3:4], axis=-1, keepdims=True)) + lam_init)
    for h in range(group):
        (ln, _), (_, l0, acc0), (_, l1, acc1) = chains[2 * h], carry[2 * h], carry[2 * h + 1]
        o = acc0 / l0 - lam * (acc1 / l1)
        o = o * lax.rsqrt(jnp.mean(o * o, axis=0, keepdims=True) + EPS)
        o_ref[:, ln] = (o.T * g_ref[...] * (1.0 - lam_init)).astype(o_ref.dtype)


def _diff_attention(qk, v, lam_vec, subln_g, batch, seq, lam_init, tq=256, tk=256, group=4, unroll=4):
    tq = min(tq, seq)
    tk = min(tk, tq)
    nq = seq // tq
    n_groups = DA_HEADS // group
    width = group * LANES
    body = functools.partial(_attn_body, tq=tq, tk=tk, lam_init=lam_init, group=group, unroll=unroll)
    return pl.pallas_call(
        body,
        grid=(batch, n_groups, nq),
        in_specs=[
            pl.BlockSpec((4, DA_HEAD_DIM), lambda b, h, i: (0, 0)),
            pl.BlockSpec((1, DA_V_DIM), lambda b, h, i: (0, 0)),
            pl.BlockSpec((tq, width), lambda b, h, i: (b * nq + i, h)),
            pl.BlockSpec((seq, width), lambda b, h, i: (b, n_groups + h)),
            pl.BlockSpec((seq, width), lambda b, h, i: (b, h)),
        ],
        out_specs=pl.BlockSpec((tq, width), lambda b, h, i: (b * nq + i, h)),
        out_shape=jax.ShapeDtypeStruct((batch * seq, DA_WIDTH), BF16),
        compiler_params=_params("parallel", "parallel", "arbitrary"),
        name="diff_attention",
    )(lam_vec, subln_g.reshape(1, DA_V_DIM), qk, qk, v)


def _rwkv_prep_body(x_ref, xprev_ref, l_ref, lprev_ref, mu_ref, mul_ref, w0_ref, a0_ref, kk_ref, ka_ref,
                    wup_ref, aup_ref, gup_ref,
                    r_out, k_out, v_out, lw_out, kk_out, a_out, g_out, *, ts):
    first = pl.program_id(1) == 0
    row = lax.broadcasted_iota(jnp.int32, (ts, 1), 0)

    def token_shift(cur_ref, prev_ref, mu):
        cur = cur_ref[0]
        last = jnp.where(first, 0.0, prev_ref[0][7:8, :])
        prev = jnp.where(row == 0, last, pltpu.roll(cur, 1, 0))
        return cur + (prev - cur) * mu

    xm = token_shift(x_ref, xprev_ref, mu_ref[...])
    xl = token_shift(l_ref, lprev_ref, mul_ref[...])
    r = xm[:, :RW_WIDTH]
    k = xm[:, RW_WIDTH:2 * RW_WIDTH]
    v = xm[:, 2 * RW_WIDTH:]
    z = -(w0_ref[...] + _dot_bf16(jnp.tanh(xl), wup_ref[...]))
    softplus = jnp.maximum(z, 0.0) + jnp.log(1.0 + jnp.exp(-jnp.abs(z)))
    w_log = -softplus - 0.5
    a = _sigmoid(a0_ref[...] + _dot_bf16(xl, aup_ref[...]))
    g = _dot_bf16(_sigmoid(xl), gup_ref[...])
    r_out[0] = r
    k_out[0] = k * (1.0 + (a - 1.0) * ka_ref[...])
    v_out[0] = v
    lw_out[0] = -jnp.exp(w_log)
    kk_out[0] = k * kk_ref[...]
    a_out[0] = a
    g_out[0] = g


def _rwkv_prep(prw, plora, mu, w0, w_up, a0, a_up, g_up, k_k, k_a, ts=256):
    b, s, w3 = prw.shape
    ts = min(ts, s)
    wl = plora.shape[-1]
    pad_rows = lambda m, lo: jnp.zeros((wl, RW_WIDTH), F32).at[lo:lo + m.shape[0]].set(m)
    wup = pad_rows(w_up, 0)
    aup = pad_rows(a_up, DECAY_LORA)
    gup = pad_rows(g_up, DECAY_LORA + AAA_LORA)
    mu_main = mu[:w3].reshape(1, w3)
    mu_lora = jnp.zeros((1, wl), F32).at[0, :RW_LORA].set(mu[w3:])
    row = lambda t: t.reshape(1, RW_WIDTH)
    blk = lambda w: pl.BlockSpec((1, ts, w), lambda bi, i: (bi, i, 0))
    prev = lambda w: pl.BlockSpec((1, 8, w), lambda bi, i: (bi, jnp.maximum(i * (ts // 8) - 1, 0), 0))
    full = lambda r, c: pl.BlockSpec((r, c), lambda bi, i: (0, 0))
    out_sds = jax.ShapeDtypeStruct((b, s, RW_WIDTH), F32)
    return pl.pallas_call(
        functools.partial(_rwkv_prep_body, ts=ts),
        grid=(b, s // ts),
        in_specs=[blk(w3), prev(w3), blk(wl), prev(wl), full(1, w3), full(1, wl),
                  full(1, RW_WIDTH), full(1, RW_WIDTH), full(1, RW_WIDTH), full(1, RW_WIDTH),
                  full(wl, RW_WIDTH), full(wl, RW_WIDTH), full(wl, RW_WIDTH)],
        out_specs=[blk(RW_WIDTH)] * 7,
        out_shape=[out_sds] * 7,
        compiler_params=_params("parallel", "parallel"),
        name="rwkv_prep",
    )(prw, prw, plora, plora, mu_main, mu_lora, row(w0), row(a0), row(k_k), row(k_a), wup, aup, gup)


def _cumsum_rows(ones_lower_bf16, x):
    hi = x.astype(BF16)
    rem = x - hi.astype(F32)
    mid = rem.astype(BF16)
    lo = (rem - mid.astype(F32)).astype(BF16)
    width = x.shape[1]
    out = jnp.dot(ones_lower_bf16, jnp.concatenate([hi, mid, lo], axis=1), preferred_element_type=F32)
    return out[:, :width] + out[:, width:2 * width] + out[:, 2 * width:]


def _rwkv_chunk_body(r_ref, k_ref, v_ref, lw_ref, kk_ref, a_ref, g_ref, rk_ref, lnw_ref, lnb_ref,
                     o_ref, st_ref, *, chunk, n_chunks, n_pairs, group):
    @pl.when(pl.program_id(1) == 0)
    def _():
        st_ref[...] = jnp.zeros_like(st_ref)

    c = chunk
    head = RW_HEAD
    lane = lax.broadcasted_iota(jnp.int32, (1, LANES), 1)
    head0 = lane < head
    row_c = lax.broadcasted_iota(jnp.int32, (c, LANES), 0)
    col_c = lax.broadcasted_iota(jnp.int32, (c, LANES), 1)
    col_in_head = jnp.where(col_c < head, col_c, col_c - head)
    strict = col_in_head < row_c
    lower = col_in_head <= row_c
    row_p = lax.broadcasted_iota(jnp.int32, (LANES, LANES), 0)
    col_p = lax.broadcasted_iota(jnp.int32, (LANES, LANES), 1)
    eye = jnp.where(row_p == col_p, 1.0, 0.0).astype(F32)
    same_head = (row_p < head) == (col_p < head)
    ones_lower = jnp.where(lax.broadcasted_iota(jnp.int32, (c, c), 1) <= lax.broadcasted_iota(jnp.int32, (c, c), 0),
                           1.0, 0.0).astype(BF16)
    zeros_c = jnp.zeros((c, LANES), BF16)
    n_doublings = int(math.log2(c)) - 1

    def only0(x):
        return jnp.where(head0, x, jnp.zeros_like(x))

    def only1(x):
        return jnp.where(head0, jnp.zeros_like(x), x)

    def head_sum(x):
        s0 = jnp.sum(only0(x), axis=-1, keepdims=True)
        s1 = jnp.sum(only1(x), axis=-1, keepdims=True)
        return jnp.where(head0, s0, s1)

    def each(fn, *lists):
        return [fn(*args) for args in zip(*lists)]

    def chunk_group(rows, pairs):
        lanes = [slice(p * LANES, (p + 1) * LANES) for p in pairs]
        load = lambda ref: [ref[0, rows, ln] for ln in lanes]
        r, k, v, lw, kkr, ag = load(r_ref), load(k_ref), load(v_ref), load(lw_ref), load(kk_ref), load(a_ref)
        kk = each(lambda t: t / jnp.maximum(jnp.sqrt(head_sum(t * t)), 1e-12), kkr)
        bm = each(lambda a, b: a * b, kk, ag)
        cw = each(lambda t: _cumsum_rows(ones_lower, t), lw)
        w_inv = each(lambda t: jnp.exp(-t), cw)
        b_t = each(lambda a, b: (a * b).astype(BF16), bm, w_inv)
        k_t = each(lambda a, b: (a * b).astype(BF16), k, w_inv)
        lhs = each(lambda kk_, cw_, lw_, r_: jnp.concatenate([-kk_ * jnp.exp(cw_ - lw_), r_ * jnp.exp(cw_)],
                                                             axis=0).astype(BF16), kk, cw, lw, r)
        g0 = each(lambda l, b, k_: lax.dot_general(only0(l), jnp.concatenate([b, k_], axis=0), _NT,
                                                   preferred_element_type=F32), lhs, b_t, k_t)
        g1 = each(lambda l, b, k_: lax.dot_general(only1(l), jnp.concatenate([k_, b], axis=0), _NT,
                                                   preferred_element_type=F32), lhs, b_t, k_t)
        top0 = each(lambda g_: jnp.where(strict, g_[:c], 0.0), g0)
        top1 = each(lambda g_: jnp.where(strict, g_[:c], 0.0), g1)
        bot = each(lambda a, b: jnp.concatenate([jnp.where(lower, a[c:], 0.0), jnp.where(lower, b[c:], 0.0)],
                                                axis=1).astype(BF16), g0, g1)
        power = each(lambda a, b: jnp.concatenate([only0(a), only1(b)], axis=0), top0, top1)
        inv = each(lambda n: eye + n, power)
        power = each(lambda t: jnp.dot(t.astype(BF16), t.astype(BF16), preferred_element_type=F32), power)
        for d in range(n_doublings):
            if d + 1 < n_doublings:
                both = each(lambda t, pw: jnp.dot(jnp.concatenate([t, pw], axis=0).astype(BF16), pw.astype(BF16),
                                                  preferred_element_type=F32), inv, power)
                inv = each(lambda t, bt: t + bt[:LANES], inv, both)
                power = each(lambda bt: bt[LANES:], both)
            else:
                inv = each(lambda t, pw: t + jnp.dot(t.astype(BF16), pw.astype(BF16), preferred_element_type=F32),
                           inv, power)
        st = [st_ref[p] for p in pairs]
        from_state = each(lambda l, s_: lax.dot_general(l, s_.astype(BF16), _NT, preferred_element_type=F32),
                          lhs, st)
        vb = each(lambda t: t.astype(BF16), v)
        x = each(lambda fs, a, b, v_: fs[:c] + jnp.dot(
            jnp.concatenate([a, b], axis=1).astype(BF16),
            jnp.concatenate([zeros_c, only0(v_), only1(v_), zeros_c], axis=0), preferred_element_type=F32),
            from_state, top0, top1, vb)
        tu = each(lambda t, x_: jnp.dot(t.astype(BF16), jnp.concatenate([only0(x_), only1(x_)], axis=0).astype(BF16),
                                        preferred_element_type=F32), inv, x)
        ub = each(lambda t: (t[:c] + t[c:]).astype(BF16), tu)
        y = each(lambda fs, b, u_, v_: fs[c:] + jnp.dot(
            b, jnp.concatenate([only0(u_), only0(v_), only1(v_), only1(u_)], axis=0), preferred_element_type=F32),
            from_state, bot, ub, vb)
        w_end = each(lambda t: t[c - 1:c], cw)
        tail = each(lambda e, t: jnp.exp(e - t), w_end, cw)
        upd = each(lambda u_, v_, bm_, k_, tl: lax.dot_general(
            jnp.concatenate([u_, v_], axis=0),
            jnp.concatenate([(bm_ * tl).astype(BF16), (k_ * tl).astype(BF16)], axis=0), _TN,
            preferred_element_type=F32), ub, vb, bm, k, tail)
        for p, s_, e, up in zip(pairs, st, w_end, upd):
            st_ref[p] = s_ * jnp.exp(e) + jnp.where(same_head, up, 0.0)
        for ln, y_, r_, k_, v_ in zip(lanes, y, r, k, v):
            yc = y_ - head_sum(y_) * (1.0 / head)
            var = head_sum(yc * yc) * (1.0 / head)
            yn = yc * lax.rsqrt(var + RW_GN_EPS) * lnw_ref[:, ln] + lnb_ref[:, ln]
            bonus = head_sum(r_ * k_ * rk_ref[:, ln]) * v_
            o_ref[rows, ln] = ((yn + bonus) * g_ref[0, rows, ln]).astype(o_ref.dtype)

    for ci in range(n_chunks):
        for p0 in range(0, n_pairs, group):
            chunk_group(slice(ci * c, (ci + 1) * c), list(range(p0, min(p0 + group, n_pairs))))


def _rwkv_chunked(r, k, v, lw, kk, a, g, r_k, ln_w, ln_b, chunk=RW_CHUNK, block=RW_CHUNK, group=8):
    b, s, w = r.shape
    chunk = min(chunk, s)
    block = min(block, s)
    nb = s // block
    n_pairs = w // LANES
    blk = pl.BlockSpec((1, block, w), lambda bi, i: (bi, i, 0))
    par = pl.BlockSpec((1, w), lambda bi, i: (0, 0))
    return pl.pallas_call(
        functools.partial(_rwkv_chunk_body, chunk=chunk, n_chunks=block // chunk, n_pairs=n_pairs, group=group),
        grid=(b, nb),
        in_specs=[blk] * 7 + [par] * 3,
        out_specs=pl.BlockSpec((block, w), lambda bi, i: (bi * nb + i, 0)),
        out_shape=jax.ShapeDtypeStruct((b * s, w), BF16),
        scratch_shapes=[pltpu.VMEM((n_pairs, LANES, LANES), F32)],
        compiler_params=_params("parallel", "arbitrary"),
        name="rwkv_chunk",
    )(r, k, v, lw, kk, a, g, r_k.reshape(1, w), ln_w.reshape(1, w), ln_b.reshape(1, w))


def _hgrn_body(lbraw_ref, ng_ref, q_ref, f_ref, i_ref, g_ref, o_ref, st_ref, *, block, layer, group):
    @pl.when(pl.program_id(2) == 0)
    def _():
        st_ref[...] = jnp.zeros_like(st_ref)

    lbraw = lbraw_ref[...]
    e = jnp.exp(lbraw - jnp.max(lbraw, axis=0, keepdims=True))
    soft = e / jnp.sum(e, axis=0, keepdims=True)
    lb_all = jnp.sum(soft[:layer + 1], axis=0, keepdims=True) - soft[0:1]
    ng = ng_ref[...]

    c, sub = HG_CHUNK, HG_SUB
    n_sub = c // sub
    row = lax.broadcasted_iota(jnp.int32, (c, c), 0)
    col = lax.broadcasted_iota(jnp.int32, (c, c), 1)
    ones_lower = jnp.where(col <= row, 1.0, 0.0).astype(BF16)
    ones_sq = jnp.ones((LANES, LANES), BF16)
    ri = lax.broadcasted_iota(jnp.int32, (c, 1), 0)
    rs = lax.broadcasted_iota(jnp.int32, (sub, 1), 0)
    zeros_sub = jnp.zeros((sub, LANES), F32)
    off_row = lax.broadcasted_iota(jnp.int32, (c, (n_sub - 1) * c), 0) // sub
    off_col = lax.broadcasted_iota(jnp.int32, (c, (n_sub - 1) * c), 1) // c
    off_keep = off_row == off_col + 1
    lanes = [slice(h * LANES, (h + 1) * LANES) for h in range(group)]
    lb = [lb_all[:, ln] for ln in lanes]

    def each(fn, *lists):
        return [fn(*args) for args in zip(*lists)]

    def diag_pieces(q, k, b):
        pieces = []
        for i in range(n_sub):
            sl = slice(sub * i, sub * (i + 1))
            qi, ki, bi = q[sl], k[sl], b[sl] * LOG2_E
            for s in range(sub):
                decay = jnp.exp2(jnp.where(rs >= s, bi - bi[s:s + 1], NEG_BIG))
                pieces.append(qi * decay * ki[s:s + 1])
        return jnp.concatenate(pieces, axis=0)

    def diag_apply(sums, iv):
        parts = []
        for i in range(n_sub):
            od = zeros_sub
            for s in range(sub):
                r0 = (i * sub + s) * sub
                od = od + sums[r0:r0 + sub] * iv[sub * i + s:sub * i + s + 1]
            parts.append(od)
        return jnp.concatenate(parts, axis=0)

    def chunk_step(ci, carry):
        rows = pl.ds(pl.multiple_of(ci * c, c), c)
        q = [q_ref[rows, ln] for ln in lanes]
        iv = [i_ref[rows, ln] for ln in lanes]
        fg = each(lambda ln, lb_: lb_ + (1.0 - lb_) * _sigmoid(f_ref[rows, ln]), lanes, lb)
        k = each(lambda t: 1.0 - t, fg)
        b = each(lambda t: _cumsum_rows(ones_lower, jnp.log(t)), fg)
        st = [st_ref[h] for h in range(group)]
        o_inter = each(lambda q_, b_, s_: _dot_bf16(q_ * jnp.exp(b_), s_, _NT), q, b, st)

        def off_scores(q_, k_, b_):
            brefs = [b_[sub * i - 1:sub * i] for i in range(1, n_sub)]
            bref_rows = jnp.concatenate([zeros_sub] + [jnp.broadcast_to(t, (sub, LANES)) for t in brefs], axis=0)
            qh = q_ * jnp.exp(jnp.where(ri >= sub, b_ - bref_rows, NEG_BIG))
            kh = jnp.concatenate([k_ * jnp.exp(jnp.where(ri < sub * (i + 1), t - b_, NEG_BIG))
                                  for i, t in enumerate(brefs)], axis=0)
            return _dot_bf16(qh, kh, _NT)
        scores = each(off_scores, q, k, b)

        stacked = each(diag_pieces, q, k, b)
        sums = each(lambda t: jnp.dot(t.astype(BF16), ones_sq, preferred_element_type=F32), stacked)
        ivb = each(lambda t: t.astype(BF16), iv)
        o_off = each(lambda sc, v_: jnp.dot(jnp.where(off_keep, sc, 0.0).astype(BF16),
                                            jnp.concatenate([v_] * (n_sub - 1), axis=0),
                                            preferred_element_type=F32), scores, ivb)
        b_end = each(lambda t: t[c - 1:c], b)
        upd = each(lambda v_, k_, b_, e_: lax.dot_general(v_, (k_ * jnp.exp(e_ - b_)).astype(BF16), _TN,
                                                          preferred_element_type=F32), ivb, k, b, b_end)
        for h in range(group):
            st_ref[h] = st[h] * jnp.exp(b_end[h]) + upd[h]
            o = o_inter[h] + o_off[h] + diag_apply(sums[h], iv[h])
            o = o * lax.rsqrt(jnp.mean(o * o, axis=-1, keepdims=True) + EPS) * ng
            gg = g_ref[rows, lanes[h]]
            o_ref[rows, lanes[h]] = (o * (gg * _sigmoid(gg))).astype(o_ref.dtype)
        return carry

    lax.fori_loop(0, block // c, chunk_step, 0, unroll=8)


def _hgrn2(p, lb_raw, norm_g, batch, seq, layer, block=512, group=4):
    block = min(block, seq)
    nb = seq // block
    depth = lb_raw.shape[0]
    n_groups = HG_HEADS // group
    width = group * LANES
    col = lambda part: pl.BlockSpec((block, width), lambda b, h, i: (b * nb + i, part * n_groups + h))
    return pl.pallas_call(
        functools.partial(_hgrn_body, block=block, layer=layer, group=group),
        grid=(batch, n_groups, nb),
        in_specs=[pl.BlockSpec((depth, width), lambda b, h, i: (0, h)),
                  pl.BlockSpec((1, LANES), lambda b, h, i: (0, 0)),
                  col(0), col(1), col(2), col(3)],
        out_specs=pl.BlockSpec((block, width), lambda b, h, i: (b * nb + i, h)),
        out_shape=jax.ShapeDtypeStruct((batch * seq, D_MODEL), BF16),
        scratch_shapes=[pltpu.VMEM((group, LANES, LANES), F32)],
        compiler_params=_params("parallel", "parallel", "arbitrary"),
        name="hgrn2",
    )(lb_raw, norm_g.reshape(1, LANES), p, p, p, p)


def _ffn(x, h, layer, w_gate, w_up, w_down, next_g, last):
    hidden = _ffn_up(h, w_gate, w_up, layer)
    return _matmul_residual_norm([hidden], w_down, layer, x, next_g, F32 if last else BF16, emit_x=not last, tm=256)


def _even_mixer(x, h, batch, seq, layer, j, w_in, w_out, ffn_g, lam_vec, subln_g, mu, w0, w_up, a0, a_up, g_up,
                k_k, k_a, r_k, ln_w, ln_b):
    qk = _qk_proj(h, w_in, j, seq)
    v = _matmul(h, w_in, j, 2 * DA_WIDTH, DA_WIDTH, BF16)
    lam_init = 0.8 - 0.6 * math.exp(-0.3 * layer)
    a_out = _diff_attention(qk, v, lam_vec, subln_g, batch, seq, lam_init)

    rw0 = 3 * DA_WIDTH
    prw = _matmul(h, w_in, j, rw0, 3 * RW_WIDTH, F32)
    w_lora = jnp.pad(w_in[j, :, rw0 + 3 * RW_WIDTH:], ((0, 0), (0, RW_LORA_PAD - RW_LORA)))
    plora = _matmul(h, w_lora[None], 0, 0, RW_LORA_PAD, F32)
    tok = _rwkv_prep(prw.reshape(batch, seq, -1), plora.reshape(batch, seq, -1), mu, w0, w_up, a0, a_up, g_up,
                     k_k, k_a)
    b_out = _rwkv_chunked(*tok, r_k, ln_w, ln_b)
    return _matmul_residual_norm([a_out, b_out], w_out, j, x, ffn_g, BF16)


def _odd_mixer(x, h, batch, seq, layer, j, w_in, w_out, ffn_g, lb_raw, hg_norm_g):
    p = _matmul(h, w_in, j, 0, w_in.shape[-1], F32, tm=2048)
    o = _hgrn2(p, lb_raw, hg_norm_g, batch, seq, layer)
    return _matmul_residual_norm([o], w_out, j, x, ffn_g, BF16)


def kernel(x, attn_norm_g, ffn_norm_g, final_norm_g, even_w_in, even_w_out, da_lambda, da_subln_g, rw_mu, rw_w0, rw_w_up, rw_a0, rw_a_up, rw_g_up, rw_k_k, rw_k_a, rw_r_k, rw_ln_w, rw_ln_b, odd_w_in, odd_w_out, hg_lower_bound, hg_norm_g, ffn_w_gate, ffn_w_up, ffn_w_down):
    batch, seq, d = x.shape
    depth = attn_norm_g.shape[0]
    xs = x.reshape(batch * seq, d)
    w_down, w_out_even, w_out_odd = (t.astype(BF16) for t in (ffn_w_down, even_w_out, odd_w_out))
    h = _rmsnorm(xs, attn_norm_g[0], BF16)
    for layer in range(depth):
        j = layer // 2
        if layer % 2 == 0:
            xs, h = _even_mixer(xs, h, batch, seq, layer, j, even_w_in, w_out_even, ffn_norm_g[layer], da_lambda[j],
                                da_subln_g[j], rw_mu[j], rw_w0[j], rw_w_up[j], rw_a0[j], rw_a_up[j], rw_g_up[j],
                                rw_k_k[j], rw_k_a[j], rw_r_k[j], rw_ln_w[j], rw_ln_b[j])
        else:
            xs, h = _odd_mixer(xs, h, batch, seq, layer, j, odd_w_in, w_out_odd, ffn_norm_g[layer],
                               hg_lower_bound, hg_norm_g[j])
        last = layer == depth - 1
        next_g = final_norm_g if last else attn_norm_g[layer + 1]
        xs, h = _ffn(xs, h, layer, ffn_w_gate, ffn_w_up, w_down, next_g, last)
    return h.reshape(batch, seq, d)
```

```python
import functools
import math

import jax
import jax.numpy as jnp
from jax import lax
from jax.experimental import pallas as pl
from jax.experimental.pallas import tpu as pltpu

D_MODEL = 2048
EPS = 1e-6

DA_WIDTH = D_MODEL // 2
DA_HEADS = 8
DA_HEAD_DIM = DA_WIDTH // DA_HEADS // 2
DA_V_DIM = 2 * DA_HEAD_DIM
ROPE_DIM = DA_HEAD_DIM // 4
ROPE_THETA = 500000.0

RW_WIDTH = D_MODEL // 2
RW_HEAD = 64
RW_HEADS = RW_WIDTH // RW_HEAD
DECAY_LORA = 64
AAA_LORA = 64
GATE_LORA = 32
RW_LORA = DECAY_LORA + AAA_LORA + GATE_LORA
RW_LORA_PAD = 256
RW_GN_EPS = 64e-5
RW_CHUNK = 64

HG_EXPAND = 128
HG_HEADS = D_MODEL // HG_EXPAND
HG_CHUNK = 64
HG_SUB = 8

LANES = 128
NEG_BIG = -1e30
LOG2_E = 1.0 / math.log(2.0)
VMEM_LIMIT = 56 * 1024 * 1024

F32 = jnp.float32
BF16 = jnp.bfloat16

_NN = (((1,), (0,)), ((), ()))
_NT = (((1,), (1,)), ((), ()))
_TN = (((0,), (0,)), ((), ()))


def _dot_bf16(a, b, dims=_NN):
    return lax.dot_general(a.astype(BF16), b.astype(BF16), dims, preferred_element_type=F32)


def _sigmoid(x):
    return 1.0 / (1.0 + jnp.exp(-x))


def _params(*sem):
    return pltpu.CompilerParams(dimension_semantics=sem, vmem_limit_bytes=VMEM_LIMIT)


def _rmsnorm_body(x_ref, g_ref, o_ref):
    x = x_ref[...]
    o_ref[...] = (x * lax.rsqrt(jnp.mean(x * x, axis=-1, keepdims=True) + EPS) * g_ref[...]).astype(o_ref.dtype)


def _rmsnorm(x2d, g, out_dtype, tm=512):
    t, d = x2d.shape
    tm = min(tm, t)
    return pl.pallas_call(
        _rmsnorm_body,
        grid=(t // tm,),
        in_specs=[pl.BlockSpec((tm, d), lambda i: (i, 0)), pl.BlockSpec((1, d), lambda i: (0, 0))],
        out_specs=pl.BlockSpec((tm, d), lambda i: (i, 0)),
        out_shape=jax.ShapeDtypeStruct((t, d), out_dtype),
        compiler_params=_params("parallel"),
        name="rmsnorm",
    )(x2d, g.reshape(1, d))


def _mm_body(a_ref, w_ref, o_ref):
    o_ref[...] = jnp.dot(a_ref[...], w_ref[...].astype(BF16), preferred_element_type=F32).astype(o_ref.dtype)


def _matmul(a, w, layer, col0, n, out_dtype, tm=1024, tn=512):
    m, k = a.shape
    tm, tn = min(tm, m), min(tn, n)
    c0 = col0 // tn
    return pl.pallas_call(
        _mm_body,
        grid=(m // tm, n // tn),
        in_specs=[pl.BlockSpec((tm, k), lambda i, j: (i, 0)),
                  pl.BlockSpec((None, k, tn), lambda i, j: (layer, 0, c0 + j))],
        out_specs=pl.BlockSpec((tm, tn), lambda i, j: (i, j)),
        out_shape=jax.ShapeDtypeStruct((m, n), out_dtype),
        compiler_params=_params("parallel", "arbitrary"),
        name="matmul",
    )(a, w)


def _norm_rows(x, g):
    return x * lax.rsqrt(jnp.mean(x * x, axis=-1, keepdims=True) + EPS) * g


def _res_norm_body(*refs, n_lhs, emit_x):
    a_refs, w_refs = refs[:n_lhs], refs[n_lhs:2 * n_lhs]
    r_ref, g_ref = refs[2 * n_lhs:2 * n_lhs + 2]
    outs = refs[2 * n_lhs + 2:]
    x = r_ref[...]
    for a_ref, w_ref in zip(a_refs, w_refs):
        x = x + jnp.dot(a_ref[...], w_ref[...], preferred_element_type=F32)
    if emit_x:
        outs[0][...] = x
    outs[-1][...] = _norm_rows(x, g_ref[...]).astype(outs[-1].dtype)


def _matmul_residual_norm(lhs, w, layer, res, g, norm_dtype, emit_x=True, tm=512):
    m, n = res.shape
    k = lhs[0].shape[1]
    tm = min(tm, m)
    row = lambda width: pl.BlockSpec((tm, width), lambda i: (i, 0))
    w_specs = [pl.BlockSpec((None, k, n), lambda i, p=p: (layer, p, 0), pipeline_mode=pl.Buffered(1))
               for p in range(len(lhs))]
    out_specs = ([row(n)] if emit_x else []) + [row(n)]
    out_shape = ([jax.ShapeDtypeStruct((m, n), F32)] if emit_x else []) + [jax.ShapeDtypeStruct((m, n), norm_dtype)]
    outs = pl.pallas_call(
        functools.partial(_res_norm_body, n_lhs=len(lhs), emit_x=emit_x),
        grid=(m // tm,),
        in_specs=[row(k)] * len(lhs) + w_specs + [row(n), pl.BlockSpec((1, n), lambda i: (0, 0))],
        out_specs=out_specs,
        out_shape=out_shape,
        compiler_params=_params("parallel"),
        name="matmul_residual_norm",
    )(*lhs, *([w] * len(lhs)), res, g.reshape(1, n))
    return outs if emit_x else (None, outs[0])


def _ffn_up_body(h_ref, wg_ref, wu_ref, o_ref):
    h = h_ref[...]
    gate = jnp.dot(h, wg_ref[...].astype(BF16), preferred_element_type=F32)
    up = jnp.dot(h, wu_ref[...].astype(BF16), preferred_element_type=F32)
    o_ref[...] = (gate * _sigmoid(gate) * up).astype(o_ref.dtype)


def _ffn_up(h, wg, wu, layer, tm=2048, tn=512):
    m, k = h.shape
    n = wg.shape[-1]
    tm, tn = min(tm, m), min(tn, n)
    wspec = pl.BlockSpec((None, k, tn), lambda i, j: (layer, 0, j))
    return pl.pallas_call(
        _ffn_up_body,
        grid=(m // tm, n // tn),
        in_specs=[pl.BlockSpec((tm, k), lambda i, j: (i, 0)), wspec, wspec],
        out_specs=pl.BlockSpec((tm, tn), lambda i, j: (i, j)),
        out_shape=jax.ShapeDtypeStruct((m, n), BF16),
        compiler_params=_params("parallel", "arbitrary"),
        name="ffn_up",
    )(h, wg, wu)


def _qk_rope_body(h_ref, w_ref, cos_ref, sin_lo_ref, sin_hi_ref, o_ref, *, n_q_tiles, tn, q_scale, row_parts):
    w = w_ref[...].astype(BF16)
    scale = jnp.where(pl.program_id(1) < n_q_tiles, q_scale, 1.0).astype(F32)
    half = ROPE_DIM // 2
    tm = h_ref.shape[0]
    rows_per_part = tm // row_parts
    accs = [jnp.dot(h_ref[r0:r0 + rows_per_part, :], w, preferred_element_type=F32)
            for r0 in range(0, tm, rows_per_part)]
    for p, acc in enumerate(accs):
        rows = slice(p * rows_per_part, (p + 1) * rows_per_part)
        cos = cos_ref[rows, :]
        sin_lo = sin_lo_ref[rows, :]
        sin_hi = sin_hi_ref[rows, :]
        for g in range(tn // LANES):
            x = acc[:, g * LANES:(g + 1) * LANES]
            y = x * cos + pltpu.roll(x, half, 1) * sin_hi + pltpu.roll(x, LANES - half, 1) * sin_lo
            o_ref[rows, g * LANES:(g + 1) * LANES] = (y * scale).astype(o_ref.dtype)


def _rope_tables(seq):
    pos = jnp.arange(seq, dtype=F32)
    inv_freq = 1.0 / (ROPE_THETA ** (jnp.arange(0, ROPE_DIM, 2, dtype=F32) / ROPE_DIM))
    ang = pos[:, None] * inv_freq[None, :]
    cos, sin = jnp.cos(ang), jnp.sin(ang)
    half = ROPE_DIM // 2
    pad = jnp.zeros((seq, DA_HEAD_DIM - ROPE_DIM), F32)
    zer = jnp.zeros((seq, half), F32)
    cos_c = jnp.concatenate([cos, cos, pad + 1.0], axis=1)
    sin_lo = jnp.concatenate([-sin, zer, pad], axis=1)
    sin_hi = jnp.concatenate([zer, sin, pad], axis=1)
    rep = lambda t: jnp.concatenate([t, t], axis=1)
    return rep(cos_c), rep(sin_lo), rep(sin_hi)


def _qk_proj(h, w_in, layer, seq, tm=2048, tn=512):
    m, k = h.shape
    n = 2 * DA_WIDTH
    tm, tn = min(tm, seq), min(tn, n)
    cos, sin_lo, sin_hi = _rope_tables(seq)
    s_tiles = seq // tm
    body = functools.partial(_qk_rope_body, n_q_tiles=DA_WIDTH // tn, tn=tn, q_scale=DA_HEAD_DIM ** -0.5 * LOG2_E,
                             row_parts=16)
    tab = pl.BlockSpec((tm, LANES), lambda i, j: (i % s_tiles, 0))
    return pl.pallas_call(
        body,
        grid=(m // tm, n // tn),
        in_specs=[pl.BlockSpec((tm, k), lambda i, j: (i, 0)),
                  pl.BlockSpec((None, k, tn), lambda i, j: (layer, 0, j)), tab, tab, tab],
        out_specs=pl.BlockSpec((tm, tn), lambda i, j: (i, j)),
        out_shape=jax.ShapeDtypeStruct((m, n), BF16),
        compiler_params=_params("parallel", "arbitrary"),
        name="qk_proj_rope",
    )(h, w_in, cos, sin_lo, sin_hi)


def _attn_body(lam_ref, g_ref, q_ref, k_ref, v_ref, o_ref, *, tq, tk, lam_init, group, unroll):
    qi = pl.program_id(2)
    lane = lax.broadcasted_iota(jnp.int32, (1, LANES), 1)
    chains = []
    for h in range(group):
        ln = slice(h * LANES, (h + 1) * LANES)
        q = q_ref[:, ln]
        zero = jnp.zeros_like(q)
        chains.append((ln, jnp.where(lane < DA_HEAD_DIM, q, zero)))
        chains.append((ln, jnp.where(lane >= DA_HEAD_DIM, q, zero)))

    def score(c, rows, keep=None):
        ln, q = chains[c]
        s = lax.dot_general(k_ref[rows, ln], q, _NT, preferred_element_type=F32)
        if keep is not None:
            s = jnp.where(keep, s, NEG_BIG)
        return s, jnp.max(s, axis=0, keepdims=True)

    def scores(rows, keep=None):
        return tuple(score(c, rows, keep) for c in range(len(chains)))

    def absorb(rows, s, carry, next_rows=None):
        n = len(chains)
        s_next = [None] * n
        if next_rows is not None:
            s_next[0] = score(0, next_rows)
        pending = []
        for c in range(n):
            if next_rows is not None and c + 1 < n:
                s_next[c + 1] = score(c + 1, next_rows)
            m = carry[c][0]
            m_new = jnp.maximum(m, s[c][1])
            p = jnp.exp2(s[c][0] - m_new)
            pv = lax.dot_general(v_ref[rows, chains[c][0]], p.astype(BF16), _TN, preferred_element_type=F32)
            pending.append((m_new, jnp.exp2(m - m_new), jnp.sum(p, axis=0, keepdims=True), pv))
        out = tuple((mn, alpha * l + ps, alpha * acc + pv)
                    for (_, l, acc), (mn, alpha, ps, pv) in zip(carry, pending))
        return tuple(s_next), out

    def block(j):
        return pl.ds(pl.multiple_of(j * tk, tk), tk)

    init = tuple((jnp.full((1, tq), NEG_BIG, F32), jnp.zeros((1, tq), F32), jnp.zeros((LANES, tq), F32))
                 for _ in chains)
    per_q = tq // tk
    n_full = qi * per_q

    def many_blocks(t, state):
        for u in range(unroll):
            j = unroll * t + u
            state = absorb(block(j), *state, next_rows=block(j + 1))
        return state
    s, carry = lax.fori_loop(0, n_full // unroll, many_blocks, (scores(block(0)), init))

    def one_block(j, state):
        return absorb(block(j), *state, next_rows=block(j + 1))
    s, carry = lax.fori_loop(n_full // unroll * unroll, n_full, one_block, (s, carry))

    for d in range(per_q):
        key = lax.broadcasted_iota(jnp.int32, (tk, tq), 0) + d * tk
        keep = key <= lax.broadcasted_iota(jnp.int32, (tk, tq), 1)
        if d == 0:
            s = tuple((jnp.where(keep, t, NEG_BIG), jnp.max(jnp.where(keep, t, NEG_BIG), axis=0, keepdims=True))
                      for t, _ in s)
        else:
            s = scores(block(n_full + d), keep)
        _, carry = absorb(block(n_full + d), s, carry)

    lf = lam_ref[...]
    lam = (jnp.exp(jnp.sum(lf[0:1] * lf[1:2], axis=-1, keepdims=True))
           - jnp.exp(jnp.sum(lf[2:3] * lf[3:4], axis=-1, keepdims=True)) + lam_init)
    for h in range(group):
        (ln, _), (_, l0, acc0), (_, l1, acc1) = chains[2 * h], carry[2 * h], carry[2 * h + 1]
        o = acc0 / l0 - lam * (acc1 / l1)
        o = o * lax.rsqrt(jnp.mean(o * o, axis=0, keepdims=True) + EPS)
        o_ref[:, ln] = (o.T * g_ref[...] * (1.0 - lam_init)).astype(o_ref.dtype)


def _diff_attention(qk, v, lam_vec, subln_g, batch, seq, lam_init, tq=256, tk=256, group=4, unroll=4):
    tq = min(tq, seq)
    tk = min(tk, tq)
    nq = seq // tq
    n_groups = DA_HEADS // group
    width = group * LANES
    body = functools.partial(_attn_body, tq=tq, tk=tk, lam_init=lam_init, group=group, unroll=unroll)
    return pl.pallas_call(
        body,
        grid=(batch, n_groups, nq),
        in_specs=[
            pl.BlockSpec((4, DA_HEAD_DIM), lambda b, h, i: (0, 0)),
            pl.BlockSpec((1, DA_V_DIM), lambda b, h, i: (0, 0)),
            pl.BlockSpec((tq, width), lambda b, h, i: (b * nq + i, h)),
            pl.BlockSpec((seq, width), lambda b, h, i: (b, n_groups + h)),
            pl.BlockSpec((seq, width), lambda b, h, i: (b, h)),
        ],
        out_specs=pl.BlockSpec((tq, width), lambda b, h, i: (b * nq + i, h)),
        out_shape=jax.ShapeDtypeStruct((batch * seq, DA_WIDTH), BF16),
        compiler_params=_params("parallel", "parallel", "arbitrary"),
        name="diff_attention",
    )(lam_vec, subln_g.reshape(1, DA_V_DIM), qk, qk, v)


def _rwkv_prep_body(x_ref, xprev_ref, l_ref, lprev_ref, mu_ref, mul_ref, w0_ref, a0_ref, kk_ref, ka_ref,
                    wup_ref, aup_ref, gup_ref,
                    r_out, k_out, v_out, lw_out, kk_out, a_out, g_out, *, ts):
    first = pl.program_id(1) == 0
    row = lax.broadcasted_iota(jnp.int32, (ts, 1), 0)

    def token_shift(cur_ref, prev_ref, mu):
        cur = cur_ref[0]
        last = jnp.where(first, 0.0, prev_ref[0][7:8, :])
        prev = jnp.where(row == 0, last, pltpu.roll(cur, 1, 0))
        return cur + (prev - cur) * mu

    xm = token_shift(x_ref, xprev_ref, mu_ref[...])
    xl = token_shift(l_ref, lprev_ref, mul_ref[...])
    r = xm[:, :RW_WIDTH]
    k = xm[:, RW_WIDTH:2 * RW_WIDTH]
    v = xm[:, 2 * RW_WIDTH:]
    z = -(w0_ref[...] + _dot_bf16(jnp.tanh(xl), wup_ref[...]))
    softplus = jnp.maximum(z, 0.0) + jnp.log(1.0 + jnp.exp(-jnp.abs(z)))
    w_log = -softplus - 0.5
    a = _sigmoid(a0_ref[...] + _dot_bf16(xl, aup_ref[...]))
    g = _dot_bf16(_sigmoid(xl), gup_ref[...])
    r_out[0] = r.astype(r_out.dtype)
    k_out[0] = (k * (1.0 + (a - 1.0) * ka_ref[...])).astype(k_out.dtype)
    v_out[0] = v.astype(v_out.dtype)
    lw_out[0] = -jnp.exp(w_log)
    kk_out[0] = (k * kk_ref[...]).astype(kk_out.dtype)
    a_out[0] = a.astype(a_out.dtype)
    g_out[0] = g.astype(g_out.dtype)


def _rwkv_prep(prw, plora, mu, w0, w_up, a0, a_up, g_up, k_k, k_a, ts=256):
    b, s, w3 = prw.shape
    ts = min(ts, s)
    wl = plora.shape[-1]
    pad_rows = lambda m, lo: jnp.zeros((wl, RW_WIDTH), F32).at[lo:lo + m.shape[0]].set(m)
    wup = pad_rows(w_up, 0)
    aup = pad_rows(a_up, DECAY_LORA)
    gup = pad_rows(g_up, DECAY_LORA + AAA_LORA)
    mu_main = mu[:w3].reshape(1, w3)
    mu_lora = jnp.zeros((1, wl), F32).at[0, :RW_LORA].set(mu[w3:])
    row = lambda t: t.reshape(1, RW_WIDTH)
    blk = lambda w: pl.BlockSpec((1, ts, w), lambda bi, i: (bi, i, 0))
    prev = lambda w: pl.BlockSpec((1, 8, w), lambda bi, i: (bi, jnp.maximum(i * (ts // 8) - 1, 0), 0))
    full = lambda r, c: pl.BlockSpec((r, c), lambda bi, i: (0, 0))
    out_sds = lambda dt: jax.ShapeDtypeStruct((b, s, RW_WIDTH), dt)
    return pl.pallas_call(
        functools.partial(_rwkv_prep_body, ts=ts),
        grid=(b, s // ts),
        in_specs=[blk(w3), prev(w3), blk(wl), prev(wl), full(1, w3), full(1, wl),
                  full(1, RW_WIDTH), full(1, RW_WIDTH), full(1, RW_WIDTH), full(1, RW_WIDTH),
                  full(wl, RW_WIDTH), full(wl, RW_WIDTH), full(wl, RW_WIDTH)],
        out_specs=[blk(RW_WIDTH)] * 7,
        out_shape=[out_sds(F32 if i == 3 else BF16) for i in range(7)],
        compiler_params=_params("parallel", "parallel"),
        name="rwkv_prep",
    )(prw, prw, plora, plora, mu_main, mu_lora, row(w0), row(a0), row(k_k), row(k_a), wup, aup, gup)


def _cumsum_rows(ones_lower_bf16, x):
    hi = x.astype(BF16)
    rem = x - hi.astype(F32)
    mid = rem.astype(BF16)
    lo = (rem - mid.astype(F32)).astype(BF16)
    width = x.shape[1]
    out = jnp.dot(ones_lower_bf16, jnp.concatenate([hi, mid, lo], axis=1), preferred_element_type=F32)
    return out[:, :width] + out[:, width:2 * width] + out[:, 2 * width:]


def _rwkv_chunk_body(r_ref, k_ref, v_ref, lw_ref, kk_ref, a_ref, g_ref, rk_ref, lnw_ref, lnb_ref,
                     o_ref, st_ref, *, chunk, n_chunks, n_pairs, group):
    @pl.when(pl.program_id(1) == 0)
    def _():
        st_ref[...] = jnp.zeros_like(st_ref)

    c = chunk
    head = RW_HEAD
    lane = lax.broadcasted_iota(jnp.int32, (1, LANES), 1)
    head0 = lane < head
    row_c = lax.broadcasted_iota(jnp.int32, (c, LANES), 0)
    col_c = lax.broadcasted_iota(jnp.int32, (c, LANES), 1)
    col_in_head = jnp.where(col_c < head, col_c, col_c - head)
    strict = col_in_head < row_c
    lower = col_in_head <= row_c
    row_p = lax.broadcasted_iota(jnp.int32, (LANES, LANES), 0)
    col_p = lax.broadcasted_iota(jnp.int32, (LANES, LANES), 1)
    eye = jnp.where(row_p == col_p, 1.0, 0.0).astype(F32)
    same_head = (row_p < head) == (col_p < head)
    ones_lower = jnp.where(lax.broadcasted_iota(jnp.int32, (c, c), 1) <= lax.broadcasted_iota(jnp.int32, (c, c), 0),
                           1.0, 0.0).astype(BF16)
    zeros_c = jnp.zeros((c, LANES), BF16)
    n_doublings = int(math.log2(c)) - 1

    def only0(x):
        return jnp.where(head0, x, jnp.zeros_like(x))

    def only1(x):
        return jnp.where(head0, jnp.zeros_like(x), x)

    def head_sum(x):
        s0 = jnp.sum(only0(x), axis=-1, keepdims=True)
        s1 = jnp.sum(only1(x), axis=-1, keepdims=True)
        return jnp.where(head0, s0, s1)

    def each(fn, *lists):
        return [fn(*args) for args in zip(*lists)]

    def chunk_group(rows, pairs):
        lanes = [slice(p * LANES, (p + 1) * LANES) for p in pairs]
        load = lambda ref: [ref[0, rows, ln].astype(F32) for ln in lanes]
        r, k, v, lw, kkr, ag = load(r_ref), load(k_ref), load(v_ref), load(lw_ref), load(kk_ref), load(a_ref)
        kk = each(lambda t: t / jnp.maximum(jnp.sqrt(head_sum(t * t)), 1e-12), kkr)
        bm = each(lambda a, b: a * b, kk, ag)
        cw = each(lambda t: _cumsum_rows(ones_lower, t), lw)
        w_inv = each(lambda t: jnp.exp(-t), cw)
        b_t = each(lambda a, b: (a * b).astype(BF16), bm, w_inv)
        k_t = each(lambda a, b: (a * b).astype(BF16), k, w_inv)
        lhs = each(lambda kk_, cw_, lw_, r_: jnp.concatenate([-kk_ * jnp.exp(cw_ - lw_), r_ * jnp.exp(cw_)],
                                                             axis=0).astype(BF16), kk, cw, lw, r)
        g0 = each(lambda l, b, k_: lax.dot_general(only0(l), jnp.concatenate([b, k_], axis=0), _NT,
                                                   preferred_element_type=F32), lhs, b_t, k_t)
        g1 = each(lambda l, b, k_: lax.dot_general(only1(l), jnp.concatenate([k_, b], axis=0), _NT,
                                                   preferred_element_type=F32), lhs, b_t, k_t)
        top0 = each(lambda g_: jnp.where(strict, g_[:c], 0.0), g0)
        top1 = each(lambda g_: jnp.where(strict, g_[:c], 0.0), g1)
        bot = each(lambda a, b: jnp.concatenate([jnp.where(lower, a[c:], 0.0), jnp.where(lower, b[c:], 0.0)],
                                                axis=1).astype(BF16), g0, g1)
        power = each(lambda a, b: jnp.concatenate([only0(a), only1(b)], axis=0), top0, top1)
        inv = each(lambda n: eye + n, power)
        power = each(lambda t: jnp.dot(t.astype(BF16), t.astype(BF16), preferred_element_type=F32), power)
        for d in range(n_doublings):
            if d + 1 < n_doublings:
                both = each(lambda t, pw: jnp.dot(jnp.concatenate([t, pw], axis=0).astype(BF16), pw.astype(BF16),
                                                  preferred_element_type=F32), inv, power)
                inv = each(lambda t, bt: t + bt[:LANES], inv, both)
                power = each(lambda bt: bt[LANES:], both)
            else:
                inv = each(lambda t, pw: t + jnp.dot(t.astype(BF16), pw.astype(BF16), preferred_element_type=F32),
                           inv, power)
        st = [st_ref[p] for p in pairs]
        from_state = each(lambda l, s_: lax.dot_general(l, s_.astype(BF16), _NT, preferred_element_type=F32),
                          lhs, st)
        vb = each(lambda t: t.astype(BF16), v)
        x = each(lambda fs, a, b, v_: fs[:c] + jnp.dot(
            jnp.concatenate([a, b], axis=1).astype(BF16),
            jnp.concatenate([zeros_c, only0(v_), only1(v_), zeros_c], axis=0), preferred_element_type=F32),
            from_state, top0, top1, vb)
        tu = each(lambda t, x_: jnp.dot(t.astype(BF16), jnp.concatenate([only0(x_), only1(x_)], axis=0).astype(BF16),
                                        preferred_element_type=F32), inv, x)
        ub = each(lambda t: (t[:c] + t[c:]).astype(BF16), tu)
        y = each(lambda fs, b, u_, v_: fs[c:] + jnp.dot(
            b, jnp.concatenate([only0(u_), only0(v_), only1(v_), only1(u_)], axis=0), preferred_element_type=F32),
            from_state, bot, ub, vb)
        w_end = each(lambda t: t[c - 1:c], cw)
        tail = each(lambda e, t: jnp.exp(e - t), w_end, cw)
        upd = each(lambda u_, v_, bm_, k_, tl: lax.dot_general(
            jnp.concatenate([u_, v_], axis=0),
            jnp.concatenate([(bm_ * tl).astype(BF16), (k_ * tl).astype(BF16)], axis=0), _TN,
            preferred_element_type=F32), ub, vb, bm, k, tail)
        for p, s_, e, up in zip(pairs, st, w_end, upd):
            st_ref[p] = s_ * jnp.exp(e) + jnp.where(same_head, up, 0.0)
        for ln, y_, r_, k_, v_ in zip(lanes, y, r, k, v):
            yc = y_ - head_sum(y_) * (1.0 / head)
            var = head_sum(yc * yc) * (1.0 / head)
            yn = yc * lax.rsqrt(var + RW_GN_EPS) * lnw_ref[:, ln] + lnb_ref[:, ln]
            bonus = head_sum(r_ * k_ * rk_ref[:, ln]) * v_
            o_ref[rows, ln] = ((yn + bonus) * g_ref[0, rows, ln].astype(F32)).astype(o_ref.dtype)

    for ci in range(n_chunks):
        for p0 in range(0, n_pairs, group):
            chunk_group(slice(ci * c, (ci + 1) * c), list(range(p0, min(p0 + group, n_pairs))))


def _rwkv_chunked(r, k, v, lw, kk, a, g, r_k, ln_w, ln_b, chunk=RW_CHUNK, block=RW_CHUNK, group=8):
    b, s, w = r.shape
    chunk = min(chunk, s)
    block = min(block, s)
    nb = s // block
    n_pairs = w // LANES
    blk = pl.BlockSpec((1, block, w), lambda bi, i: (bi, i, 0))
    par = pl.BlockSpec((1, w), lambda bi, i: (0, 0))
    return pl.pallas_call(
        functools.partial(_rwkv_chunk_body, chunk=chunk, n_chunks=block // chunk, n_pairs=n_pairs, group=group),
        grid=(b, nb),
        in_specs=[blk] * 7 + [par] * 3,
        out_specs=pl.BlockSpec((block, w), lambda bi, i: (bi * nb + i, 0)),
        out_shape=jax.ShapeDtypeStruct((b * s, w), BF16),
        scratch_shapes=[pltpu.VMEM((n_pairs, LANES, LANES), F32)],
        compiler_params=_params("parallel", "arbitrary"),
        name="rwkv_chunk",
    )(r, k, v, lw, kk, a, g, r_k.reshape(1, w), ln_w.reshape(1, w), ln_b.reshape(1, w))


def _hgrn_body(lbraw_ref, ng_ref, q_ref, f_ref, i_ref, g_ref, o_ref, st_ref, *, block, layer, group):
    @pl.when(pl.program_id(2) == 0)
    def _():
        st_ref[...] = jnp.zeros_like(st_ref)

    lbraw = lbraw_ref[...]
    e = jnp.exp(lbraw - jnp.max(lbraw, axis=0, keepdims=True))
    soft = e / jnp.sum(e, axis=0, keepdims=True)
    lb_all = jnp.sum(soft[:layer + 1], axis=0, keepdims=True) - soft[0:1]
    ng = ng_ref[...]

    c, sub = HG_CHUNK, HG_SUB
    n_sub = c // sub
    row = lax.broadcasted_iota(jnp.int32, (c, c), 0)
    col = lax.broadcasted_iota(jnp.int32, (c, c), 1)
    ones_lower = jnp.where(col <= row, 1.0, 0.0).astype(BF16)
    ones_sq = jnp.ones((LANES, LANES), BF16)
    ri = lax.broadcasted_iota(jnp.int32, (c, 1), 0)
    rs = lax.broadcasted_iota(jnp.int32, (sub, 1), 0)
    zeros_sub = jnp.zeros((sub, LANES), F32)
    off_row = lax.broadcasted_iota(jnp.int32, (c, (n_sub - 1) * c), 0) // sub
    off_col = lax.broadcasted_iota(jnp.int32, (c, (n_sub - 1) * c), 1) // c
    off_keep = off_row == off_col + 1
    lanes = [slice(h * LANES, (h + 1) * LANES) for h in range(group)]
    lb = [lb_all[:, ln] for ln in lanes]

    def each(fn, *lists):
        return [fn(*args) for args in zip(*lists)]

    def diag_pieces(q, k, b):
        pieces = []
        for i in range(n_sub):
            sl = slice(sub * i, sub * (i + 1))
            qi, ki, bi = q[sl], k[sl], b[sl]
            for s in range(sub):
                decay = jnp.exp2(jnp.where(rs >= s, bi - bi[s:s + 1], NEG_BIG))
                pieces.append(qi * decay * ki[s:s + 1])
        return jnp.concatenate(pieces, axis=0)

    def diag_apply(sums, iv):
        parts = []
        for i in range(n_sub):
            od = zeros_sub
            for s in range(sub):
                r0 = (i * sub + s) * sub
                od = od + sums[r0:r0 + sub] * iv[sub * i + s:sub * i + s + 1]
            parts.append(od)
        return jnp.concatenate(parts, axis=0)

    def chunk_step(ci, carry):
        rows = pl.ds(pl.multiple_of(ci * c, c), c)
        q = [q_ref[rows, ln] for ln in lanes]
        iv = [i_ref[rows, ln] for ln in lanes]
        fg = each(lambda ln, lb_: lb_ + (1.0 - lb_) * _sigmoid(f_ref[rows, ln]), lanes, lb)
        k = each(lambda t: 1.0 - t, fg)
        b = each(lambda t: _cumsum_rows(ones_lower, jnp.log(t)) * LOG2_E, fg)
        st = [st_ref[h] for h in range(group)]
        o_inter = each(lambda q_, b_, s_: _dot_bf16(q_ * jnp.exp2(b_), s_, _NT), q, b, st)

        def off_scores(q_, k_, b_):
            brefs = [b_[sub * i - 1:sub * i] for i in range(1, n_sub)]
            bref_rows = jnp.concatenate([zeros_sub] + [jnp.broadcast_to(t, (sub, LANES)) for t in brefs], axis=0)
            qh = q_ * jnp.exp2(jnp.where(ri >= sub, b_ - bref_rows, NEG_BIG))
            kh = jnp.concatenate([k_ * jnp.exp2(jnp.where(ri < sub * (i + 1), t - b_, NEG_BIG))
                                  for i, t in enumerate(brefs)], axis=0)
            return _dot_bf16(qh, kh, _NT)
        scores = each(off_scores, q, k, b)

        stacked = each(diag_pieces, q, k, b)
        sums = each(lambda t: jnp.dot(t.astype(BF16), ones_sq, preferred_element_type=F32), stacked)
        ivb = each(lambda t: t.astype(BF16), iv)
        o_off = each(lambda sc, v_: jnp.dot(jnp.where(off_keep, sc, 0.0).astype(BF16),
                                            jnp.concatenate([v_] * (n_sub - 1), axis=0),
                                            preferred_element_type=F32), scores, ivb)
        b_end = each(lambda t: t[c - 1:c], b)
        upd = each(lambda v_, k_, b_, e_: lax.dot_general(v_, (k_ * jnp.exp2(e_ - b_)).astype(BF16), _TN,
                                                          preferred_element_type=F32), ivb, k, b, b_end)
        for h in range(group):
            st_ref[h] = st[h] * jnp.exp2(b_end[h]) + upd[h]
            o = o_inter[h] + o_off[h] + diag_apply(sums[h], iv[h])
            o = o * lax.rsqrt(jnp.mean(o * o, axis=-1, keepdims=True) + EPS) * ng
            gg = g_ref[rows, lanes[h]]
            o_ref[rows, lanes[h]] = (o * (gg * _sigmoid(gg))).astype(o_ref.dtype)
        return carry

    lax.fori_loop(0, block // c, chunk_step, 0, unroll=8)


def _hgrn2(p, lb_raw, norm_g, batch, seq, layer, block=512, group=4):
    block = min(block, seq)
    nb = seq // block
    depth = lb_raw.shape[0]
    n_groups = HG_HEADS // group
    width = group * LANES
    col = lambda part: pl.BlockSpec((block, width), lambda b, h, i: (b * nb + i, part * n_groups + h))
    return pl.pallas_call(
        functools.partial(_hgrn_body, block=block, layer=layer, group=group),
        grid=(batch, n_groups, nb),
        in_specs=[pl.BlockSpec((depth, width), lambda b, h, i: (0, h)),
                  pl.BlockSpec((1, LANES), lambda b, h, i: (0, 0)),
                  col(0), col(1), col(2), col(3)],
        out_specs=pl.BlockSpec((block, width), lambda b, h, i: (b * nb + i, h)),
        out_shape=jax.ShapeDtypeStruct((batch * seq, D_MODEL), BF16),
        scratch_shapes=[pltpu.VMEM((group, LANES, LANES), F32)],
        compiler_params=_params("parallel", "parallel", "arbitrary"),
        name="hgrn2",
    )(lb_raw, norm_g.reshape(1, LANES), p, p, p, p)


def _ffn(x, h, layer, w_gate, w_up, w_down, next_g, last):
    hidden = _ffn_up(h, w_gate, w_up, layer)
    return _matmul_residual_norm([hidden], w_down, layer, x, next_g, F32 if last else BF16, emit_x=not last, tm=256)


def _even_mixer(x, h, batch, seq, layer, j, w_in, w_out, ffn_g, lam_vec, subln_g, mu, w0, w_up, a0, a_up, g_up,
                k_k, k_a, r_k, ln_w, ln_b):
    qk = _qk_proj(h, w_in, j, seq)
    v = _matmul(h, w_in, j, 2 * DA_WIDTH, DA_WIDTH, BF16, tm=2048)
    lam_init = 0.8 - 0.6 * math.exp(-0.3 * layer)
    a_out = _diff_attention(qk, v, lam_vec, subln_g, batch, seq, lam_init)

    rw0 = 3 * DA_WIDTH
    prw = _matmul(h, w_in, j, rw0, 3 * RW_WIDTH, F32, tm=2048)
    w_lora = jnp.pad(w_in[j, :, rw0 + 3 * RW_WIDTH:], ((0, 0), (0, RW_LORA_PAD - RW_LORA)))
    plora = _matmul(h, w_lora[None], 0, 0, RW_LORA_PAD, F32)
    tok = _rwkv_prep(prw.reshape(batch, seq, -1), plora.reshape(batch, seq, -1), mu, w0, w_up, a0, a_up, g_up,
                     k_k, k_a)
    b_out = _rwkv_chunked(*tok, r_k, ln_w, ln_b)
    return _matmul_residual_norm([a_out, b_out], w_out, j, x, ffn_g, BF16)


def _odd_mixer(x, h, batch, seq, layer, j, w_in, w_out, ffn_g, lb_raw, hg_norm_g):
    p = _matmul(h, w_in, j, 0, w_in.shape[-1], F32, tm=2048)
    o = _hgrn2(p, lb_raw, hg_norm_g, batch, seq, layer)
    return _matmul_residual_norm([o], w_out, j, x, ffn_g, BF16)


def kernel(x, attn_norm_g, ffn_norm_g, final_norm_g, even_w_in, even_w_out, da_lambda, da_subln_g, rw_mu, rw_w0, rw_w_up, rw_a0, rw_a_up, rw_g_up, rw_k_k, rw_k_a, rw_r_k, rw_ln_w, rw_ln_b, odd_w_in, odd_w_out, hg_lower_bound, hg_norm_g, ffn_w_gate, ffn_w_up, ffn_w_down):
    batch, seq, d = x.shape
    depth = attn_norm_g.shape[0]
    xs = x.reshape(batch * seq, d)
    w_down, w_out_even, w_out_odd = (t.astype(BF16) for t in (ffn_w_down, even_w_out, odd_w_out))
    h = _rmsnorm(xs, attn_norm_g[0], BF16)
    for layer in range(depth):
        j = layer // 2
        if layer % 2 == 0:
            xs, h = _even_mixer(xs, h, batch, seq, layer, j, even_w_in, w_out_even, ffn_norm_g[layer], da_lambda[j],
                                da_subln_g[j], rw_mu[j], rw_w0[j], rw_w_up[j], rw_a0[j], rw_a_up[j], rw_g_up[j],
                                rw_k_k[j], rw_k_a[j], rw_r_k[j], rw_ln_w[j], rw_ln_b[j])
        else:
            xs, h = _odd_mixer(xs, h, batch, seq, layer, j, odd_w_in, w_out_odd, ffn_norm_g[layer],
                               hg_lower_bound, hg_norm_g[j])
        last = layer == depth - 1
        next_g = final_norm_g if last else attn_norm_g[layer + 1]
        xs, h = _ffn(xs, h, layer, ffn_w_gate, ffn_w_up, w_down, next_g, last)
    return h.reshape(batch, seq, d)
```

```python
import functools
import math

import jax
import jax.numpy as jnp
import numpy as np
from jax import lax
from jax.experimental import pallas as pl
from jax.experimental.pallas import tpu as pltpu

D_MODEL = 2048
EPS = 1e-6

DA_WIDTH = D_MODEL // 2
DA_HEADS = 8
DA_HEAD_DIM = DA_WIDTH // DA_HEADS // 2
DA_V_DIM = 2 * DA_HEAD_DIM
ROPE_DIM = DA_HEAD_DIM // 4
ROPE_THETA = 500000.0

RW_WIDTH = D_MODEL // 2
RW_HEAD = 64
RW_HEADS = RW_WIDTH // RW_HEAD
DECAY_LORA = 64
AAA_LORA = 64
GATE_LORA = 32
RW_LORA = DECAY_LORA + AAA_LORA + GATE_LORA
RW_LORA_PAD = 256
RW_GN_EPS = 64e-5
RW_CHUNK = 64

HG_EXPAND = 128
HG_HEADS = D_MODEL // HG_EXPAND
HG_CHUNK = 64
HG_SUB = 8

LANES = 128
NEG_BIG = -1e30
LOG2_E = 1.0 / math.log(2.0)
VMEM_LIMIT = 56 * 1024 * 1024

F32 = jnp.float32
BF16 = jnp.bfloat16

_NN = (((1,), (0,)), ((), ()))
_NT = (((1,), (1,)), ((), ()))
_TN = (((0,), (0,)), ((), ()))


def _dot_bf16(a, b, dims=_NN):
    return lax.dot_general(a.astype(BF16), b.astype(BF16), dims, preferred_element_type=F32)


def _sigmoid(x):
    return 1.0 / (1.0 + jnp.exp(-x))


def _params(*sem):
    return pltpu.CompilerParams(dimension_semantics=sem, vmem_limit_bytes=VMEM_LIMIT)


def _rmsnorm_body(x_ref, g_ref, o_ref):
    x = x_ref[...]
    o_ref[...] = (x * lax.rsqrt(jnp.mean(x * x, axis=-1, keepdims=True) + EPS) * g_ref[...]).astype(o_ref.dtype)


def _rmsnorm(x2d, g, out_dtype, tm=512):
    t, d = x2d.shape
    tm = min(tm, t)
    return pl.pallas_call(
        _rmsnorm_body,
        grid=(t // tm,),
        in_specs=[pl.BlockSpec((tm, d), lambda i: (i, 0)), pl.BlockSpec((1, d), lambda i: (0, 0))],
        out_specs=pl.BlockSpec((tm, d), lambda i: (i, 0)),
        out_shape=jax.ShapeDtypeStruct((t, d), out_dtype),
        compiler_params=_params("parallel"),
        name="rmsnorm",
    )(x2d, g.reshape(1, d))


def _mm_body(a_ref, w_ref, o_ref):
    o_ref[...] = jnp.dot(a_ref[...], w_ref[...].astype(BF16), preferred_element_type=F32).astype(o_ref.dtype)


def _matmul(a, w, layer, col0, n, out_dtype, tm=1024, tn=512):
    m, k = a.shape
    tm, tn = min(tm, m), min(tn, n)
    c0 = col0 // tn
    return pl.pallas_call(
        _mm_body,
        grid=(m // tm, n // tn),
        in_specs=[pl.BlockSpec((tm, k), lambda i, j: (i, 0)),
                  pl.BlockSpec((None, k, tn), lambda i, j: (layer, 0, c0 + j))],
        out_specs=pl.BlockSpec((tm, tn), lambda i, j: (i, j)),
        out_shape=jax.ShapeDtypeStruct((m, n), out_dtype),
        compiler_params=_params("parallel", "arbitrary"),
        name="matmul",
    )(a, w)


def _norm_rows(x, g):
    return x * lax.rsqrt(jnp.mean(x * x, axis=-1, keepdims=True) + EPS) * g


def _res_norm_body(*refs, n_lhs, emit_x):
    a_refs, w_refs = refs[:n_lhs], refs[n_lhs:2 * n_lhs]
    r_ref, g_ref = refs[2 * n_lhs:2 * n_lhs + 2]
    outs = refs[2 * n_lhs + 2:]
    x = r_ref[...]
    for a_ref, w_ref in zip(a_refs, w_refs):
        x = x + jnp.dot(a_ref[...], w_ref[...], preferred_element_type=F32)
    if emit_x:
        outs[0][...] = x
    outs[-1][...] = _norm_rows(x, g_ref[...]).astype(outs[-1].dtype)


def _matmul_residual_norm(lhs, w, layer, res, g, norm_dtype, emit_x=True, tm=512):
    m, n = res.shape
    k = lhs[0].shape[1]
    tm = min(tm, m)
    row = lambda width: pl.BlockSpec((tm, width), lambda i: (i, 0))
    w_specs = [pl.BlockSpec((None, k, n), lambda i, p=p: (layer, p, 0), pipeline_mode=pl.Buffered(1))
               for p in range(len(lhs))]
    out_specs = ([row(n)] if emit_x else []) + [row(n)]
    out_shape = ([jax.ShapeDtypeStruct((m, n), F32)] if emit_x else []) + [jax.ShapeDtypeStruct((m, n), norm_dtype)]
    outs = pl.pallas_call(
        functools.partial(_res_norm_body, n_lhs=len(lhs), emit_x=emit_x),
        grid=(m // tm,),
        in_specs=[row(k)] * len(lhs) + w_specs + [row(n), pl.BlockSpec((1, n), lambda i: (0, 0))],
        out_specs=out_specs,
        out_shape=out_shape,
        compiler_params=_params("parallel"),
        name="matmul_residual_norm",
    )(*lhs, *([w] * len(lhs)), res, g.reshape(1, n))
    return outs if emit_x else (None, outs[0])


def _ffn_up_body(h_ref, wg_ref, wu_ref, o_ref):
    h = h_ref[...]
    gate = jnp.dot(h, wg_ref[...].astype(BF16), preferred_element_type=F32)
    up = jnp.dot(h, wu_ref[...].astype(BF16), preferred_element_type=F32)
    o_ref[...] = (gate * _sigmoid(gate) * up).astype(o_ref.dtype)


def _ffn_up(h, wg, wu, layer, tm=2048, tn=512):
    m, k = h.shape
    n = wg.shape[-1]
    tm, tn = min(tm, m), min(tn, n)
    wspec = pl.BlockSpec((None, k, tn), lambda i, j: (layer, 0, j))
    return pl.pallas_call(
        _ffn_up_body,
        grid=(m // tm, n // tn),
        in_specs=[pl.BlockSpec((tm, k), lambda i, j: (i, 0)), wspec, wspec],
        out_specs=pl.BlockSpec((tm, tn), lambda i, j: (i, j)),
        out_shape=jax.ShapeDtypeStruct((m, n), BF16),
        compiler_params=_params("parallel", "arbitrary"),
        name="ffn_up",
    )(h, wg, wu)


def _qk_rope_body(h_ref, w_ref, cos_ref, sin_lo_ref, sin_hi_ref, o_ref, *, n_q_tiles, tn, q_scale, row_parts):
    w = w_ref[...].astype(BF16)
    scale = jnp.where(pl.program_id(1) < n_q_tiles, q_scale, 1.0).astype(F32)
    half = ROPE_DIM // 2
    tm = h_ref.shape[0]
    rows_per_part = tm // row_parts
    accs = [jnp.dot(h_ref[r0:r0 + rows_per_part, :], w, preferred_element_type=F32)
            for r0 in range(0, tm, rows_per_part)]
    for p, acc in enumerate(accs):
        rows = slice(p * rows_per_part, (p + 1) * rows_per_part)
        cos = cos_ref[rows, :]
        sin_lo = sin_lo_ref[rows, :]
        sin_hi = sin_hi_ref[rows, :]
        for g in range(tn // LANES):
            x = acc[:, g * LANES:(g + 1) * LANES]
            y = x * cos + pltpu.roll(x, half, 1) * sin_hi + pltpu.roll(x, LANES - half, 1) * sin_lo
            o_ref[rows, g * LANES:(g + 1) * LANES] = (y * scale).astype(o_ref.dtype)


def _rope_tables(seq):
    pos = np.arange(seq, dtype=np.float64)
    inv_freq = 1.0 / (ROPE_THETA ** (np.arange(0, ROPE_DIM, 2, dtype=np.float64) / ROPE_DIM))
    ang = pos[:, None] * inv_freq[None, :]
    cos, sin = np.cos(ang), np.sin(ang)
    half = ROPE_DIM // 2
    pad = np.zeros((seq, DA_HEAD_DIM - ROPE_DIM))
    zer = np.zeros((seq, half))
    cos_c = np.concatenate([cos, cos, pad + 1.0], axis=1)
    sin_lo = np.concatenate([-sin, zer, pad], axis=1)
    sin_hi = np.concatenate([zer, sin, pad], axis=1)
    rep = lambda t: jnp.asarray(np.concatenate([t, t], axis=1), dtype=F32)
    return rep(cos_c), rep(sin_lo), rep(sin_hi)


def _qk_proj(h, w_in, layer, seq, tm=2048, tn=512):
    m, k = h.shape
    n = 2 * DA_WIDTH
    tm, tn = min(tm, seq), min(tn, n)
    cos, sin_lo, sin_hi = _rope_tables(seq)
    s_tiles = seq // tm
    body = functools.partial(_qk_rope_body, n_q_tiles=DA_WIDTH // tn, tn=tn, q_scale=DA_HEAD_DIM ** -0.5 * LOG2_E,
                             row_parts=16)
    tab = pl.BlockSpec((tm, LANES), lambda i, j: (i % s_tiles, 0))
    return pl.pallas_call(
        body,
        grid=(m // tm, n // tn),
        in_specs=[pl.BlockSpec((tm, k), lambda i, j: (i, 0)),
                  pl.BlockSpec((None, k, tn), lambda i, j: (layer, 0, j)), tab, tab, tab],
        out_specs=pl.BlockSpec((tm, tn), lambda i, j: (i, j)),
        out_shape=jax.ShapeDtypeStruct((m, n), BF16),
        compiler_params=_params("parallel", "arbitrary"),
        name="qk_proj_rope",
    )(h, w_in, cos, sin_lo, sin_hi)


def _attn_body(lam_ref, g_ref, q_ref, k_ref, v_ref, o_ref, *, tq, tk, lam_init, group, unroll, lookahead):
    qi = pl.program_id(2)
    lane = lax.broadcasted_iota(jnp.int32, (1, LANES), 1)
    chains = []
    for h in range(group):
        ln = slice(h * LANES, (h + 1) * LANES)
        q = q_ref[:, ln]
        zero = jnp.zeros_like(q)
        chains.append((ln, jnp.where(lane < DA_HEAD_DIM, q, zero)))
        chains.append((ln, jnp.where(lane >= DA_HEAD_DIM, q, zero)))

    def score(c, rows, keep=None):
        ln, q = chains[c]
        s = lax.dot_general(k_ref[rows, ln], q, _NT, preferred_element_type=F32)
        if keep is not None:
            s = jnp.where(keep, s, NEG_BIG)
        return s, jnp.max(s, axis=0, keepdims=True)

    def run_blocks(blocks, carry):
        n = len(chains)
        stages = [(rows, keep, c) for rows, keep in blocks for c in range(n)]
        carry = list(carry)
        pending = None

        def apply(update):
            c, m_new, alpha, p_sum, pv = update
            _, l, acc = carry[c]
            carry[c] = (m_new, alpha * l + p_sum, alpha * acc + pv)

        issue = lambda st: score(st[2], st[0], st[1])
        queue = [issue(st) for st in stages[:lookahead]]
        for idx, (rows, _, c) in enumerate(stages):
            if idx + lookahead < len(stages):
                queue.append(issue(stages[idx + lookahead]))
            s_cur = queue.pop(0)
            m = carry[c][0]
            m_new = jnp.maximum(m, s_cur[1])
            p = jnp.exp2(s_cur[0] - m_new)
            pv = lax.dot_general(v_ref[rows, chains[c][0]], p.astype(BF16), _TN, preferred_element_type=F32)
            if pending is not None:
                apply(pending)
            pending = (c, m_new, jnp.exp2(m - m_new), jnp.sum(p, axis=0, keepdims=True), pv)
        apply(pending)
        return tuple(carry)

    def block(j):
        return pl.ds(pl.multiple_of(j * tk, tk), tk)

    init = tuple((jnp.full((1, tq), NEG_BIG, F32), jnp.zeros((1, tq), F32), jnp.zeros((LANES, tq), F32))
                 for _ in chains)
    per_q = tq // tk
    n_full = qi * per_q

    carry = lax.fori_loop(0, n_full // unroll,
                          lambda t, cr: run_blocks([(block(unroll * t + u), None) for u in range(unroll)], cr), init)
    carry = lax.fori_loop(n_full // unroll * unroll, n_full, lambda j, cr: run_blocks([(block(j), None)], cr), carry)
    diagonal = []
    for d in range(per_q):
        key = lax.broadcasted_iota(jnp.int32, (tk, tq), 0) + d * tk
        diagonal.append((block(n_full + d), key <= lax.broadcasted_iota(jnp.int32, (tk, tq), 1)))
    carry = run_blocks(diagonal, carry)

    lf = lam_ref[...]
    lam = (jnp.exp(jnp.sum(lf[0:1] * lf[1:2], axis=-1, keepdims=True))
           - jnp.exp(jnp.sum(lf[2:3] * lf[3:4], axis=-1, keepdims=True)) + lam_init)
    for h in range(group):
        (ln, _), (_, l0, acc0), (_, l1, acc1) = chains[2 * h], carry[2 * h], carry[2 * h + 1]
        o = acc0 / l0 - lam * (acc1 / l1)
        o = o * lax.rsqrt(jnp.mean(o * o, axis=0, keepdims=True) + EPS)
        o_ref[:, ln] = (o.T * g_ref[...] * (1.0 - lam_init)).astype(o_ref.dtype)


def _diff_attention(qk, v, lam_vec, subln_g, batch, seq, lam_init, tq=256, tk=256, group=4, unroll=4, lookahead=8):
    tq = min(tq, seq)
    tk = min(tk, tq)
    nq = seq // tq
    n_groups = DA_HEADS // group
    width = group * LANES
    body = functools.partial(_attn_body, tq=tq, tk=tk, lam_init=lam_init, group=group, unroll=unroll,
                             lookahead=lookahead)
    return pl.pallas_call(
        body,
        grid=(batch, n_groups, nq),
        in_specs=[
            pl.BlockSpec((4, DA_HEAD_DIM), lambda b, h, i: (0, 0)),
            pl.BlockSpec((1, DA_V_DIM), lambda b, h, i: (0, 0)),
            pl.BlockSpec((tq, width), lambda b, h, i: (b * nq + i, h)),
            pl.BlockSpec((seq, width), lambda b, h, i: (b, n_groups + h)),
            pl.BlockSpec((seq, width), lambda b, h, i: (b, h)),
        ],
        out_specs=pl.BlockSpec((tq, width), lambda b, h, i: (b * nq + i, h)),
        out_shape=jax.ShapeDtypeStruct((batch * seq, DA_WIDTH), BF16),
        compiler_params=_params("parallel", "parallel", "arbitrary"),
        name="diff_attention",
    )(lam_vec, subln_g.reshape(1, DA_V_DIM), qk, qk, v)


def _rwkv_prep_body(x_ref, xprev_ref, l_ref, lprev_ref, mu_ref, mul_ref, w0_ref, a0_ref, kk_ref, ka_ref,
                    wup_ref, aup_ref, gup_ref,
                    r_out, k_out, v_out, lw_out, kk_out, a_out, g_out, *, ts):
    first = pl.program_id(1) == 0
    row = lax.broadcasted_iota(jnp.int32, (ts, 1), 0)

    def token_shift(cur_ref, prev_ref, mu):
        cur = cur_ref[0]
        last = jnp.where(first, 0.0, prev_ref[0][7:8, :])
        prev = jnp.where(row == 0, last, pltpu.roll(cur, 1, 0))
        return cur + (prev - cur) * mu

    xm = token_shift(x_ref, xprev_ref, mu_ref[...])
    xl = token_shift(l_ref, lprev_ref, mul_ref[...])
    r = xm[:, :RW_WIDTH]
    k = xm[:, RW_WIDTH:2 * RW_WIDTH]
    v = xm[:, 2 * RW_WIDTH:]
    z = -(w0_ref[...] + _dot_bf16(jnp.tanh(xl), wup_ref[...]))
    softplus = jnp.maximum(z, 0.0) + jnp.log(1.0 + jnp.exp(-jnp.abs(z)))
    w_log = -softplus - 0.5
    a = _sigmoid(a0_ref[...] + _dot_bf16(xl, aup_ref[...]))
    g = _dot_bf16(_sigmoid(xl), gup_ref[...])
    r_out[0] = r.astype(r_out.dtype)
    k_out[0] = (k * (1.0 + (a - 1.0) * ka_ref[...])).astype(k_out.dtype)
    v_out[0] = v.astype(v_out.dtype)
    lw_out[0] = -jnp.exp(w_log)
    kk_out[0] = (k * kk_ref[...]).astype(kk_out.dtype)
    a_out[0] = a.astype(a_out.dtype)
    g_out[0] = g.astype(g_out.dtype)


def _rwkv_prep(prw, plora, mu, w0, w_up, a0, a_up, g_up, k_k, k_a, ts=256):
    b, s, w3 = prw.shape
    ts = min(ts, s)
    wl = plora.shape[-1]
    pad_rows = lambda m, lo: jnp.zeros((wl, RW_WIDTH), F32).at[lo:lo + m.shape[0]].set(m)
    wup = pad_rows(w_up, 0)
    aup = pad_rows(a_up, DECAY_LORA)
    gup = pad_rows(g_up, DECAY_LORA + AAA_LORA)
    mu_main = mu[:w3].reshape(1, w3)
    mu_lora = jnp.zeros((1, wl), F32).at[0, :RW_LORA].set(mu[w3:])
    row = lambda t: t.reshape(1, RW_WIDTH)
    blk = lambda w: pl.BlockSpec((1, ts, w), lambda bi, i: (bi, i, 0))
    prev = lambda w: pl.BlockSpec((1, 8, w), lambda bi, i: (bi, jnp.maximum(i * (ts // 8) - 1, 0), 0))
    full = lambda r, c: pl.BlockSpec((r, c), lambda bi, i: (0, 0))
    out_sds = lambda dt: jax.ShapeDtypeStruct((b, s, RW_WIDTH), dt)
    return pl.pallas_call(
        functools.partial(_rwkv_prep_body, ts=ts),
        grid=(b, s // ts),
        in_specs=[blk(w3), prev(w3), blk(wl), prev(wl), full(1, w3), full(1, wl),
                  full(1, RW_WIDTH), full(1, RW_WIDTH), full(1, RW_WIDTH), full(1, RW_WIDTH),
                  full(wl, RW_WIDTH), full(wl, RW_WIDTH), full(wl, RW_WIDTH)],
        out_specs=[blk(RW_WIDTH)] * 7,
        out_shape=[out_sds(F32 if i == 3 else BF16) for i in range(7)],
        compiler_params=_params("parallel", "parallel"),
        name="rwkv_prep",
    )(prw, prw, plora, plora, mu_main, mu_lora, row(w0), row(a0), row(k_k), row(k_a), wup, aup, gup)


def _cumsum_rows(ones_lower_bf16, x):
    hi = x.astype(BF16)
    rem = x - hi.astype(F32)
    mid = rem.astype(BF16)
    lo = (rem - mid.astype(F32)).astype(BF16)
    width = x.shape[1]
    out = jnp.dot(ones_lower_bf16, jnp.concatenate([hi, mid, lo], axis=1), preferred_element_type=F32)
    return out[:, :width] + out[:, width:2 * width] + out[:, 2 * width:]


def _rwkv_chunk_body(r_ref, k_ref, v_ref, lw_ref, kk_ref, a_ref, g_ref, rk_ref, lnw_ref, lnb_ref,
                     o_ref, st_ref, *, chunk, n_chunks, n_pairs, group):
    @pl.when(pl.program_id(1) == 0)
    def _():
        st_ref[...] = jnp.zeros_like(st_ref)

    c = chunk
    head = RW_HEAD
    lane = lax.broadcasted_iota(jnp.int32, (1, LANES), 1)
    head0 = lane < head
    row_c = lax.broadcasted_iota(jnp.int32, (c, LANES), 0)
    col_c = lax.broadcasted_iota(jnp.int32, (c, LANES), 1)
    col_in_head = jnp.where(col_c < head, col_c, col_c - head)
    strict = col_in_head < row_c
    lower = col_in_head <= row_c
    row_p = lax.broadcasted_iota(jnp.int32, (LANES, LANES), 0)
    col_p = lax.broadcasted_iota(jnp.int32, (LANES, LANES), 1)
    eye = jnp.where(row_p == col_p, 1.0, 0.0).astype(F32)
    same_head = (row_p < head) == (col_p < head)
    ones_lower = jnp.where(lax.broadcasted_iota(jnp.int32, (c, c), 1) <= lax.broadcasted_iota(jnp.int32, (c, c), 0),
                           1.0, 0.0).astype(BF16)
    zeros_c = jnp.zeros((c, LANES), BF16)
    n_doublings = int(math.log2(c)) - 1

    def only0(x):
        return jnp.where(head0, x, jnp.zeros_like(x))

    def only1(x):
        return jnp.where(head0, jnp.zeros_like(x), x)

    def head_sum(x):
        s0 = jnp.sum(only0(x), axis=-1, keepdims=True)
        s1 = jnp.sum(only1(x), axis=-1, keepdims=True)
        return jnp.where(head0, s0, s1)

    def each(fn, *lists):
        return [fn(*args) for args in zip(*lists)]

    def chunk_group(rows, pairs):
        lanes = [slice(p * LANES, (p + 1) * LANES) for p in pairs]
        load = lambda ref: [ref[0, rows, ln].astype(F32) for ln in lanes]
        r, k, v, lw, kkr, ag = load(r_ref), load(k_ref), load(v_ref), load(lw_ref), load(kk_ref), load(a_ref)
        kk = each(lambda t: t / jnp.maximum(jnp.sqrt(head_sum(t * t)), 1e-12), kkr)
        bm = each(lambda a, b: a * b, kk, ag)
        cw = each(lambda t: _cumsum_rows(ones_lower, t), lw)
        w_inv = each(lambda t: jnp.exp(-t), cw)
        b_t = each(lambda a, b: (a * b).astype(BF16), bm, w_inv)
        k_t = each(lambda a, b: (a * b).astype(BF16), k, w_inv)
        lhs = each(lambda kk_, cw_, lw_, r_: jnp.concatenate([-kk_ * jnp.exp(cw_ - lw_), r_ * jnp.exp(cw_)],
                                                             axis=0).astype(BF16), kk, cw, lw, r)
        g0 = each(lambda l, b, k_: lax.dot_general(only0(l), jnp.concatenate([b, k_], axis=0), _NT,
                                                   preferred_element_type=F32), lhs, b_t, k_t)
        g1 = each(lambda l, b, k_: lax.dot_general(only1(l), jnp.concatenate([k_, b], axis=0), _NT,
                                                   preferred_element_type=F32), lhs, b_t, k_t)
        top0 = each(lambda g_: jnp.where(strict, g_[:c], 0.0), g0)
        top1 = each(lambda g_: jnp.where(strict, g_[:c], 0.0), g1)
        bot = each(lambda a, b: jnp.concatenate([jnp.where(lower, a[c:], 0.0), jnp.where(lower, b[c:], 0.0)],
                                                axis=1).astype(BF16), g0, g1)
        power = each(lambda a, b: jnp.concatenate([only0(a), only1(b)], axis=0), top0, top1)
        inv = each(lambda n: eye + n, power)
        power = each(lambda t: jnp.dot(t.astype(BF16), t.astype(BF16), preferred_element_type=F32), power)
        for d in range(n_doublings):
            if d + 1 < n_doublings:
                both = each(lambda t, pw: jnp.dot(jnp.concatenate([t, pw], axis=0).astype(BF16), pw.astype(BF16),
                                                  preferred_element_type=F32), inv, power)
                inv = each(lambda t, bt: t + bt[:LANES], inv, both)
                power = each(lambda bt: bt[LANES:], both)
            else:
                inv = each(lambda t, pw: t + jnp.dot(t.astype(BF16), pw.astype(BF16), preferred_element_type=F32),
                           inv, power)
        st = [st_ref[p] for p in pairs]
        from_state = each(lambda l, s_: lax.dot_general(l, s_.astype(BF16), _NT, preferred_element_type=F32),
                          lhs, st)
        vb = each(lambda t: t.astype(BF16), v)
        x = each(lambda fs, a, b, v_: fs[:c] + jnp.dot(
            jnp.concatenate([a, b], axis=1).astype(BF16),
            jnp.concatenate([zeros_c, only0(v_), only1(v_), zeros_c], axis=0), preferred_element_type=F32),
            from_state, top0, top1, vb)
        tu = each(lambda t, x_: jnp.dot(t.astype(BF16), jnp.concatenate([only0(x_), only1(x_)], axis=0).astype(BF16),
                                        preferred_element_type=F32), inv, x)
        ub = each(lambda t: (t[:c] + t[c:]).astype(BF16), tu)
        y = each(lambda fs, b, u_, v_: fs[c:] + jnp.dot(
            b, jnp.concatenate([only0(u_), only0(v_), only1(v_), only1(u_)], axis=0), preferred_element_type=F32),
            from_state, bot, ub, vb)
        w_end = each(lambda t: t[c - 1:c], cw)
        tail = each(lambda e, t: jnp.exp(e - t), w_end, cw)
        upd = each(lambda u_, v_, bm_, k_, tl: lax.dot_general(
            jnp.concatenate([u_, v_], axis=0),
            jnp.concatenate([(bm_ * tl).astype(BF16), (k_ * tl).astype(BF16)], axis=0), _TN,
            preferred_element_type=F32), ub, vb, bm, k, tail)
        for p, s_, e, up in zip(pairs, st, w_end, upd):
            st_ref[p] = s_ * jnp.exp(e) + jnp.where(same_head, up, 0.0)
        for ln, y_, r_, k_, v_ in zip(lanes, y, r, k, v):
            yc = y_ - head_sum(y_) * (1.0 / head)
            var = head_sum(yc * yc) * (1.0 / head)
            yn = yc * lax.rsqrt(var + RW_GN_EPS) * lnw_ref[:, ln] + lnb_ref[:, ln]
            bonus = head_sum(r_ * k_ * rk_ref[:, ln]) * v_
            o_ref[rows, ln] = ((yn + bonus) * g_ref[0, rows, ln].astype(F32)).astype(o_ref.dtype)

    for ci in range(n_chunks):
        for p0 in range(0, n_pairs, group):
            chunk_group(slice(ci * c, (ci + 1) * c), list(range(p0, min(p0 + group, n_pairs))))


def _rwkv_chunked(r, k, v, lw, kk, a, g, r_k, ln_w, ln_b, chunk=RW_CHUNK, block=RW_CHUNK, group=8):
    b, s, w = r.shape
    chunk = min(chunk, s)
    block = min(block, s)
    nb = s // block
    n_pairs = w // LANES
    blk = pl.BlockSpec((1, block, w), lambda bi, i: (bi, i, 0))
    par = pl.BlockSpec((1, w), lambda bi, i: (0, 0))
    return pl.pallas_call(
        functools.partial(_rwkv_chunk_body, chunk=chunk, n_chunks=block // chunk, n_pairs=n_pairs, group=group),
        grid=(b, nb),
        in_specs=[blk] * 7 + [par] * 3,
        out_specs=pl.BlockSpec((block, w), lambda bi, i: (bi * nb + i, 0)),
        out_shape=jax.ShapeDtypeStruct((b * s, w), BF16),
        scratch_shapes=[pltpu.VMEM((n_pairs, LANES, LANES), F32)],
        compiler_params=_params("parallel", "arbitrary"),
        name="rwkv_chunk",
    )(r, k, v, lw, kk, a, g, r_k.reshape(1, w), ln_w.reshape(1, w), ln_b.reshape(1, w))


def _hgrn_body(lbraw_ref, ng_ref, q_ref, f_ref, i_ref, g_ref, o_ref, st_ref, *, block, layer, group):
    @pl.when(pl.program_id(2) == 0)
    def _():
        st_ref[...] = jnp.zeros_like(st_ref)

    lbraw = lbraw_ref[...]
    e = jnp.exp(lbraw - jnp.max(lbraw, axis=0, keepdims=True))
    soft = e / jnp.sum(e, axis=0, keepdims=True)
    lb_all = jnp.sum(soft[:layer + 1], axis=0, keepdims=True) - soft[0:1]
    ng = ng_ref[...]

    c, sub = HG_CHUNK, HG_SUB
    n_sub = c // sub
    row = lax.broadcasted_iota(jnp.int32, (c, c), 0)
    col = lax.broadcasted_iota(jnp.int32, (c, c), 1)
    ones_lower = jnp.where(col <= row, 1.0, 0.0).astype(BF16)
    ones_sq = jnp.ones((LANES, LANES), BF16)
    ri = lax.broadcasted_iota(jnp.int32, (c, 1), 0)
    rs = lax.broadcasted_iota(jnp.int32, (sub, 1), 0)
    zeros_sub = jnp.zeros((sub, LANES), F32)
    off_row = lax.broadcasted_iota(jnp.int32, (c, (n_sub - 1) * c), 0) // sub
    off_col = lax.broadcasted_iota(jnp.int32, (c, (n_sub - 1) * c), 1) // c
    off_keep = off_row == off_col + 1
    lanes = [slice(h * LANES, (h + 1) * LANES) for h in range(group)]
    lb = [lb_all[:, ln] for ln in lanes]

    def each(fn, *lists):
        return [fn(*args) for args in zip(*lists)]

    def diag_pieces(q, k, b):
        pieces = []
        for i in range(n_sub):
            sl = slice(sub * i, sub * (i + 1))
            qi, ki, bi = q[sl], k[sl], b[sl]
            for s in range(sub):
                decay = jnp.exp2(jnp.where(rs >= s, bi - bi[s:s + 1], NEG_BIG))
                pieces.append(qi * decay * ki[s:s + 1])
        return jnp.concatenate(pieces, axis=0)

    def diag_apply(sums, iv):
        parts = []
        for i in range(n_sub):
            od = zeros_sub
            for s in range(sub):
                r0 = (i * sub + s) * sub
                od = od + sums[r0:r0 + sub] * iv[sub * i + s:sub * i + s + 1]
            parts.append(od)
        return jnp.concatenate(parts, axis=0)

    def chunk_step(ci, carry):
        rows = pl.ds(pl.multiple_of(ci * c, c), c)
        q = [q_ref[rows, ln] for ln in lanes]
        iv = [i_ref[rows, ln] for ln in lanes]
        fg = each(lambda ln, lb_: lb_ + (1.0 - lb_) * _sigmoid(f_ref[rows, ln]), lanes, lb)
        k = each(lambda t: 1.0 - t, fg)
        b = each(lambda t: _cumsum_rows(ones_lower, jnp.log(t)) * LOG2_E, fg)
        st = [st_ref[h] for h in range(group)]
        o_inter = each(lambda q_, b_, s_: _dot_bf16(q_ * jnp.exp2(b_), s_, _NT), q, b, st)

        def off_scores(q_, k_, b_):
            brefs = [b_[sub * i - 1:sub * i] for i in range(1, n_sub)]
            bref_rows = jnp.concatenate([zeros_sub] + [jnp.broadcast_to(t, (sub, LANES)) for t in brefs], axis=0)
            qh = q_ * jnp.exp2(jnp.where(ri >= sub, b_ - bref_rows, NEG_BIG))
            kh = jnp.concatenate([k_ * jnp.exp2(jnp.where(ri < sub * (i + 1), t - b_, NEG_BIG))
                                  for i, t in enumerate(brefs)], axis=0)
            return _dot_bf16(qh, kh, _NT)
        scores = each(off_scores, q, k, b)

        stacked = each(diag_pieces, q, k, b)
        sums = each(lambda t: jnp.dot(t.astype(BF16), ones_sq, preferred_element_type=F32), stacked)
        ivb = each(lambda t: t.astype(BF16), iv)
        o_off = each(lambda sc, v_: jnp.dot(jnp.where(off_keep, sc, 0.0).astype(BF16),
                                            jnp.concatenate([v_] * (n_sub - 1), axis=0),
                                            preferred_element_type=F32), scores, ivb)
        b_end = each(lambda t: t[c - 1:c], b)
        upd = each(lambda v_, k_, b_, e_: lax.dot_general(v_, (k_ * jnp.exp2(e_ - b_)).astype(BF16), _TN,
                                                          preferred_element_type=F32), ivb, k, b, b_end)
        for h in range(group):
            st_ref[h] = st[h] * jnp.exp2(b_end[h]) + upd[h]
            o = o_inter[h] + o_off[h] + diag_apply(sums[h], iv[h])
            o = o * lax.rsqrt(jnp.mean(o * o, axis=-1, keepdims=True) + EPS) * ng
            gg = g_ref[rows, lanes[h]]
            o_ref[rows, lanes[h]] = (o * (gg * _sigmoid(gg))).astype(o_ref.dtype)
        return carry

    lax.fori_loop(0, block // c, chunk_step, 0, unroll=8)


def _hgrn2(p, lb_raw, norm_g, batch, seq, layer, block=512, group=4):
    block = min(block, seq)
    nb = seq // block
    depth = lb_raw.shape[0]
    n_groups = HG_HEADS // group
    width = group * LANES
    col = lambda part: pl.BlockSpec((block, width), lambda b, h, i: (b * nb + i, part * n_groups + h))
    return pl.pallas_call(
        functools.partial(_hgrn_body, block=block, layer=layer, group=group),
        grid=(batch, n_groups, nb),
        in_specs=[pl.BlockSpec((depth, width), lambda b, h, i: (0, h)),
                  pl.BlockSpec((1, LANES), lambda b, h, i: (0, 0)),
                  col(0), col(1), col(2), col(3)],
        out_specs=pl.BlockSpec((block, width), lambda b, h, i: (b * nb + i, h)),
        out_shape=jax.ShapeDtypeStruct((batch * seq, D_MODEL), BF16),
        scratch_shapes=[pltpu.VMEM((group, LANES, LANES), F32)],
        compiler_params=_params("parallel", "parallel", "arbitrary"),
        name="hgrn2",
    )(lb_raw, norm_g.reshape(1, LANES), p, p, p, p)


def _ffn(x, h, layer, w_gate, w_up, w_down, next_g, last):
    hidden = _ffn_up(h, w_gate, w_up, layer)
    return _matmul_residual_norm([hidden], w_down, layer, x, next_g, F32 if last else BF16, emit_x=not last, tm=256)


def _even_mixer(x, h, batch, seq, layer, j, w_in, w_out, ffn_g, lam_vec, subln_g, mu, w0, w_up, a0, a_up, g_up,
                k_k, k_a, r_k, ln_w, ln_b):
    qk = _qk_proj(h, w_in, j, seq)
    v = _matmul(h, w_in, j, 2 * DA_WIDTH, DA_WIDTH, BF16, tm=2048)
    lam_init = 0.8 - 0.6 * math.exp(-0.3 * layer)
    a_out = _diff_attention(qk, v, lam_vec, subln_g, batch, seq, lam_init)

    rw0 = 3 * DA_WIDTH
    prw = _matmul(h, w_in, j, rw0, 3 * RW_WIDTH, F32, tm=2048)
    w_lora = jnp.pad(w_in[j, :, rw0 + 3 * RW_WIDTH:], ((0, 0), (0, RW_LORA_PAD - RW_LORA)))
    plora = _matmul(h, w_lora[None], 0, 0, RW_LORA_PAD, F32)
    tok = _rwkv_prep(prw.reshape(batch, seq, -1), plora.reshape(batch, seq, -1), mu, w0, w_up, a0, a_up, g_up,
                     k_k, k_a)
    b_out = _rwkv_chunked(*tok, r_k, ln_w, ln_b)
    return _matmul_residual_norm([a_out, b_out], w_out, j, x, ffn_g, BF16)


def _odd_mixer(x, h, batch, seq, layer, j, w_in, w_out, ffn_g, lb_raw, hg_norm_g):
    p = _matmul(h, w_in, j, 0, w_in.shape[-1], F32, tm=2048)
    o = _hgrn2(p, lb_raw, hg_norm_g, batch, seq, layer)
    return _matmul_residual_norm([o], w_out, j, x, ffn_g, BF16)


def kernel(x, attn_norm_g, ffn_norm_g, final_norm_g, even_w_in, even_w_out, da_lambda, da_subln_g, rw_mu, rw_w0, rw_w_up, rw_a0, rw_a_up, rw_g_up, rw_k_k, rw_k_a, rw_r_k, rw_ln_w, rw_ln_b, odd_w_in, odd_w_out, hg_lower_bound, hg_norm_g, ffn_w_gate, ffn_w_up, ffn_w_down):
    batch, seq, d = x.shape
    depth = attn_norm_g.shape[0]
    xs = x.reshape(batch * seq, d)
    w_down, w_out_even, w_out_odd = (t.astype(BF16) for t in (ffn_w_down, even_w_out, odd_w_out))
    h = _rmsnorm(xs, attn_norm_g[0], BF16)
    for layer in range(depth):
        j = layer // 2
        if layer % 2 == 0:
            xs, h = _even_mixer(xs, h, batch, seq, layer, j, even_w_in, w_out_even, ffn_norm_g[layer], da_lambda[j],
                                da_subln_g[j], rw_mu[j], rw_w0[j], rw_w_up[j], rw_a0[j], rw_a_up[j], rw_g_up[j],
                                rw_k_k[j], rw_k_a[j], rw_r_k[j], rw_ln_w[j], rw_ln_b[j])
        else:
            xs, h = _odd_mixer(xs, h, batch, seq, layer, j, odd_w_in, w_out_odd, ffn_norm_g[layer],
                               hg_lower_bound, hg_norm_g[j])
        last = layer == depth - 1
        next_g = final_norm_g if last else attn_norm_g[layer + 1]
        xs, h = _ffn(xs, h, layer, ffn_w_gate, ffn_w_up, w_down, next_g, last)
    return h.reshape(batch, seq, d)
```

```python
import functools
import math

import jax
import jax.numpy as jnp
import numpy as np
from jax import lax
from jax.experimental import pallas as pl
from jax.experimental.pallas import tpu as pltpu

D_MODEL = 2048
EPS = 1e-6

DA_WIDTH = D_MODEL // 2
DA_HEADS = 8
DA_HEAD_DIM = DA_WIDTH // DA_HEADS // 2
DA_V_DIM = 2 * DA_HEAD_DIM
ROPE_DIM = DA_HEAD_DIM // 4
ROPE_THETA = 500000.0

RW_WIDTH = D_MODEL // 2
RW_HEAD = 64
RW_HEADS = RW_WIDTH // RW_HEAD
DECAY_LORA = 64
AAA_LORA = 64
GATE_LORA = 32
RW_LORA = DECAY_LORA + AAA_LORA + GATE_LORA
RW_LORA_PAD = 256
RW_GN_EPS = 64e-5
RW_CHUNK = 64

HG_EXPAND = 128
HG_HEADS = D_MODEL // HG_EXPAND
HG_CHUNK = 64
HG_SUB = 8

LANES = 128
NEG_BIG = -1e30
LOG2_E = 1.0 / math.log(2.0)
VMEM_LIMIT = 56 * 1024 * 1024

F32 = jnp.float32
BF16 = jnp.bfloat16

_NN = (((1,), (0,)), ((), ()))
_NT = (((1,), (1,)), ((), ()))
_TN = (((0,), (0,)), ((), ()))


def _dot_bf16(a, b, dims=_NN):
    return lax.dot_general(a.astype(BF16), b.astype(BF16), dims, preferred_element_type=F32)


def _sigmoid(x):
    return 1.0 / (1.0 + jnp.exp(-x))


def _params(*sem):
    return pltpu.CompilerParams(dimension_semantics=sem, vmem_limit_bytes=VMEM_LIMIT)


def _rmsnorm_body(x_ref, g_ref, o_ref):
    x = x_ref[...]
    o_ref[...] = (x * lax.rsqrt(jnp.mean(x * x, axis=-1, keepdims=True) + EPS) * g_ref[...]).astype(o_ref.dtype)


def _rmsnorm(x2d, g, out_dtype, tm=512):
    t, d = x2d.shape
    tm = min(tm, t)
    return pl.pallas_call(
        _rmsnorm_body,
        grid=(t // tm,),
        in_specs=[pl.BlockSpec((tm, d), lambda i: (i, 0)), pl.BlockSpec((1, d), lambda i: (0, 0))],
        out_specs=pl.BlockSpec((tm, d), lambda i: (i, 0)),
        out_shape=jax.ShapeDtypeStruct((t, d), out_dtype),
        compiler_params=_params("parallel"),
        name="rmsnorm",
    )(x2d, g.reshape(1, d))


def _mm_body(a_ref, w_ref, o_ref):
    o_ref[...] = jnp.dot(a_ref[...], w_ref[...].astype(BF16), preferred_element_type=F32).astype(o_ref.dtype)


def _matmul(a, w, layer, col0, n, out_dtype, tm=1024, tn=512):
    m, k = a.shape
    tm, tn = min(tm, m), min(tn, n)
    c0 = col0 // tn
    return pl.pallas_call(
        _mm_body,
        grid=(m // tm, n // tn),
        in_specs=[pl.BlockSpec((tm, k), lambda i, j: (i, 0)),
                  pl.BlockSpec((None, k, tn), lambda i, j: (layer, 0, c0 + j))],
        out_specs=pl.BlockSpec((tm, tn), lambda i, j: (i, j)),
        out_shape=jax.ShapeDtypeStruct((m, n), out_dtype),
        compiler_params=_params("parallel", "arbitrary"),
        name="matmul",
    )(a, w)


def _norm_rows(x, g):
    return x * lax.rsqrt(jnp.mean(x * x, axis=-1, keepdims=True) + EPS) * g


def _res_norm_body(*refs, n_lhs, emit_x):
    a_refs, w_refs = refs[:n_lhs], refs[n_lhs:2 * n_lhs]
    r_ref, g_ref = refs[2 * n_lhs:2 * n_lhs + 2]
    outs = refs[2 * n_lhs + 2:]
    x = r_ref[...]
    for a_ref, w_ref in zip(a_refs, w_refs):
        x = x + jnp.dot(a_ref[...], w_ref[...], preferred_element_type=F32)
    if emit_x:
        outs[0][...] = x
    outs[-1][...] = _norm_rows(x, g_ref[...]).astype(outs[-1].dtype)


def _matmul_residual_norm(lhs, w, layer, res, g, norm_dtype, emit_x=True, tm=512):
    m, n = res.shape
    k = lhs[0].shape[1]
    tm = min(tm, m)
    row = lambda width: pl.BlockSpec((tm, width), lambda i: (i, 0))
    w_specs = [pl.BlockSpec((None, k, n), lambda i, p=p: (layer, p, 0), pipeline_mode=pl.Buffered(1))
               for p in range(len(lhs))]
    out_specs = ([row(n)] if emit_x else []) + [row(n)]
    out_shape = ([jax.ShapeDtypeStruct((m, n), F32)] if emit_x else []) + [jax.ShapeDtypeStruct((m, n), norm_dtype)]
    outs = pl.pallas_call(
        functools.partial(_res_norm_body, n_lhs=len(lhs), emit_x=emit_x),
        grid=(m // tm,),
        in_specs=[row(k)] * len(lhs) + w_specs + [row(n), pl.BlockSpec((1, n), lambda i: (0, 0))],
        out_specs=out_specs,
        out_shape=out_shape,
        compiler_params=_params("parallel"),
        name="matmul_residual_norm",
    )(*lhs, *([w] * len(lhs)), res, g.reshape(1, n))
    return outs if emit_x else (None, outs[0])


def _ffn_up_body(h_ref, wg_ref, wu_ref, o_ref):
    h = h_ref[...]
    gate = jnp.dot(h, wg_ref[...].astype(BF16), preferred_element_type=F32)
    up = jnp.dot(h, wu_ref[...].astype(BF16), preferred_element_type=F32)
    o_ref[...] = (gate * _sigmoid(gate) * up).astype(o_ref.dtype)


def _ffn_up(h, wg, wu, layer, tm=2048, tn=512):
    m, k = h.shape
    n = wg.shape[-1]
    tm, tn = min(tm, m), min(tn, n)
    wspec = pl.BlockSpec((None, k, tn), lambda i, j: (layer, 0, j))
    return pl.pallas_call(
        _ffn_up_body,
        grid=(m // tm, n // tn),
        in_specs=[pl.BlockSpec((tm, k), lambda i, j: (i, 0)), wspec, wspec],
        out_specs=pl.BlockSpec((tm, tn), lambda i, j: (i, j)),
        out_shape=jax.ShapeDtypeStruct((m, n), BF16),
        compiler_params=_params("parallel", "arbitrary"),
        name="ffn_up",
    )(h, wg, wu)


def _qk_rope_body(h_ref, w_ref, cos_ref, sin_lo_ref, sin_hi_ref, o_ref, *, n_q_tiles, tn, q_scale, row_parts):
    w = w_ref[...].astype(BF16)
    scale = jnp.where(pl.program_id(1) < n_q_tiles, q_scale, 1.0).astype(F32)
    half = ROPE_DIM // 2
    tm = h_ref.shape[0]
    rows_per_part = tm // row_parts
    accs = [jnp.dot(h_ref[r0:r0 + rows_per_part, :], w, preferred_element_type=F32)
            for r0 in range(0, tm, rows_per_part)]
    for p, acc in enumerate(accs):
        rows = slice(p * rows_per_part, (p + 1) * rows_per_part)
        cos = cos_ref[rows, :]
        sin_lo = sin_lo_ref[rows, :]
        sin_hi = sin_hi_ref[rows, :]
        for g in range(tn // LANES):
            x = acc[:, g * LANES:(g + 1) * LANES]
            y = x * cos + pltpu.roll(x, half, 1) * sin_hi + pltpu.roll(x, LANES - half, 1) * sin_lo
            o_ref[rows, g * LANES:(g + 1) * LANES] = (y * scale).astype(o_ref.dtype)


def _rope_tables(seq):
    pos = np.arange(seq, dtype=np.float64)
    inv_freq = 1.0 / (ROPE_THETA ** (np.arange(0, ROPE_DIM, 2, dtype=np.float64) / ROPE_DIM))
    ang = pos[:, None] * inv_freq[None, :]
    cos, sin = np.cos(ang), np.sin(ang)
    half = ROPE_DIM // 2
    pad = np.zeros((seq, DA_HEAD_DIM - ROPE_DIM))
    zer = np.zeros((seq, half))
    cos_c = np.concatenate([cos, cos, pad + 1.0], axis=1)
    sin_lo = np.concatenate([-sin, zer, pad], axis=1)
    sin_hi = np.concatenate([zer, sin, pad], axis=1)
    rep = lambda t: jnp.asarray(np.concatenate([t, t], axis=1), dtype=F32)
    return rep(cos_c), rep(sin_lo), rep(sin_hi)


def _qk_proj(h, w_in, layer, seq, tm=2048, tn=512):
    m, k = h.shape
    n = 2 * DA_WIDTH
    tm, tn = min(tm, seq), min(tn, n)
    cos, sin_lo, sin_hi = _rope_tables(seq)
    s_tiles = seq // tm
    body = functools.partial(_qk_rope_body, n_q_tiles=DA_WIDTH // tn, tn=tn, q_scale=DA_HEAD_DIM ** -0.5 * LOG2_E,
                             row_parts=16)
    tab = pl.BlockSpec((tm, LANES), lambda i, j: (i % s_tiles, 0))
    return pl.pallas_call(
        body,
        grid=(m // tm, n // tn),
        in_specs=[pl.BlockSpec((tm, k), lambda i, j: (i, 0)),
                  pl.BlockSpec((None, k, tn), lambda i, j: (layer, 0, j)), tab, tab, tab],
        out_specs=pl.BlockSpec((tm, tn), lambda i, j: (i, j)),
        out_shape=jax.ShapeDtypeStruct((m, n), BF16),
        compiler_params=_params("parallel", "arbitrary"),
        name="qk_proj_rope",
    )(h, w_in, cos, sin_lo, sin_hi)


def _attn_body(lam_ref, g_ref, q_ref, k_ref, v_ref, o_ref, *, tq, tk, lam_init, group, unroll, lookahead):
    qi = pl.program_id(2)
    lane = lax.broadcasted_iota(jnp.int32, (1, LANES), 1)
    chains = []
    for h in range(group):
        ln = slice(h * LANES, (h + 1) * LANES)
        q = q_ref[:, ln]
        zero = jnp.zeros_like(q)
        chains.append((ln, jnp.where(lane < DA_HEAD_DIM, q, zero)))
        chains.append((ln, jnp.where(lane >= DA_HEAD_DIM, q, zero)))

    def score(c, rows, keep=None):
        ln, q = chains[c]
        s = lax.dot_general(k_ref[rows, ln], q, _NT, preferred_element_type=F32)
        if keep is not None:
            s = jnp.where(keep, s, NEG_BIG)
        return s, jnp.max(s, axis=0, keepdims=True)

    def run_blocks(blocks, carry):
        n = len(chains)
        stages = [(rows, keep, c) for rows, keep in blocks for c in range(n)]
        carry = list(carry)
        pending = None

        def apply(update):
            c, m_new, alpha, p_sum, pv = update
            _, l, acc = carry[c]
            carry[c] = (m_new, alpha * l + p_sum, alpha * acc + pv)

        issue = lambda st: score(st[2], st[0], st[1])
        queue = [issue(st) for st in stages[:lookahead]]
        for idx, (rows, _, c) in enumerate(stages):
            if idx + lookahead < len(stages):
                queue.append(issue(stages[idx + lookahead]))
            s_cur = queue.pop(0)
            m = carry[c][0]
            m_new = jnp.maximum(m, s_cur[1])
            p = jnp.exp2(s_cur[0] - m_new)
            pv = lax.dot_general(v_ref[rows, chains[c][0]], p.astype(BF16), _TN, preferred_element_type=F32)
            if pending is not None:
                apply(pending)
            pending = (c, m_new, jnp.exp2(m - m_new), jnp.sum(p, axis=0, keepdims=True), pv)
        apply(pending)
        return tuple(carry)

    def block(j):
        return pl.ds(pl.multiple_of(j * tk, tk), tk)

    init = tuple((jnp.full((1, tq), NEG_BIG, F32), jnp.zeros((1, tq), F32), jnp.zeros((LANES, tq), F32))
                 for _ in chains)
    per_q = tq // tk
    n_full = qi * per_q

    carry = lax.fori_loop(0, n_full // unroll,
                          lambda t, cr: run_blocks([(block(unroll * t + u), None) for u in range(unroll)], cr), init)
    carry = lax.fori_loop(n_full // unroll * unroll, n_full, lambda j, cr: run_blocks([(block(j), None)], cr), carry)
    diagonal = []
    for d in range(per_q):
        key = lax.broadcasted_iota(jnp.int32, (tk, tq), 0) + d * tk
        diagonal.append((block(n_full + d), key <= lax.broadcasted_iota(jnp.int32, (tk, tq), 1)))
    carry = run_blocks(diagonal, carry)

    lf = lam_ref[...]
    lam = (jnp.exp(jnp.sum(lf[0:1] * lf[1:2], axis=-1, keepdims=True))
           - jnp.exp(jnp.sum(lf[2:3] * lf[3:4], axis=-1, keepdims=True)) + lam_init)
    for h in range(group):
        (ln, _), (_, l0, acc0), (_, l1, acc1) = chains[2 * h], carry[2 * h], carry[2 * h + 1]
        o = acc0 / l0 - lam * (acc1 / l1)
        o = o * lax.rsqrt(jnp.mean(o * o, axis=0, keepdims=True) + EPS)
        o_ref[:, ln] = (o.T * g_ref[...] * (1.0 - lam_init)).astype(o_ref.dtype)


def _diff_attention(qk, v, lam_vec, subln_g, batch, seq, lam_init, tq=256, tk=256, group=4, unroll=4, lookahead=8):
    tq = min(tq, seq)
    tk = min(tk, tq)
    nq = seq // tq
    n_groups = DA_HEADS // group
    width = group * LANES
    body = functools.partial(_attn_body, tq=tq, tk=tk, lam_init=lam_init, group=group, unroll=unroll,
                             lookahead=lookahead)
    return pl.pallas_call(
        body,
        grid=(batch, n_groups, nq),
        in_specs=[
            pl.BlockSpec((4, DA_HEAD_DIM), lambda b, h, i: (0, 0)),
            pl.BlockSpec((1, DA_V_DIM), lambda b, h, i: (0, 0)),
            pl.BlockSpec((tq, width), lambda b, h, i: (b * nq + i, h)),
            pl.BlockSpec((seq, width), lambda b, h, i: (b, n_groups + h)),
            pl.BlockSpec((seq, width), lambda b, h, i: (b, h)),
        ],
        out_specs=pl.BlockSpec((tq, width), lambda b, h, i: (b * nq + i, h)),
        out_shape=jax.ShapeDtypeStruct((batch * seq, DA_WIDTH), BF16),
        compiler_params=_params("parallel", "parallel", "arbitrary"),
        name="diff_attention",
    )(lam_vec, subln_g.reshape(1, DA_V_DIM), qk, qk, v)


def _rwkv_prep_body(x_ref, xprev_ref, l_ref, lprev_ref, mu_ref, mul_ref, w0_ref, a0_ref, kk_ref, ka_ref,
                    wup_ref, aup_ref, gup_ref,
                    r_out, k_out, v_out, lw_out, kk_out, a_out, g_out, *, ts):
    first = pl.program_id(1) == 0
    row = lax.broadcasted_iota(jnp.int32, (ts, 1), 0)

    def token_shift(cur_ref, prev_ref, mu):
        cur = cur_ref[0]
        last = jnp.where(first, 0.0, prev_ref[0][7:8, :])
        prev = jnp.where(row == 0, last, pltpu.roll(cur, 1, 0))
        return cur + (prev - cur) * mu

    xm = token_shift(x_ref, xprev_ref, mu_ref[...])
    xl = token_shift(l_ref, lprev_ref, mul_ref[...])
    r = xm[:, :RW_WIDTH]
    k = xm[:, RW_WIDTH:2 * RW_WIDTH]
    v = xm[:, 2 * RW_WIDTH:]
    z = -(w0_ref[...] + _dot_bf16(jnp.tanh(xl), wup_ref[...]))
    softplus = jnp.maximum(z, 0.0) + jnp.log(1.0 + jnp.exp(-jnp.abs(z)))
    w_log = -softplus - 0.5
    a = _sigmoid(a0_ref[...] + _dot_bf16(xl, aup_ref[...]))
    g = _dot_bf16(_sigmoid(xl), gup_ref[...])
    r_out[0] = r.astype(r_out.dtype)
    k_out[0] = (k * (1.0 + (a - 1.0) * ka_ref[...])).astype(k_out.dtype)
    v_out[0] = v.astype(v_out.dtype)
    lw_out[0] = -jnp.exp(w_log)
    kk_out[0] = (k * kk_ref[...]).astype(kk_out.dtype)
    a_out[0] = a.astype(a_out.dtype)
    g_out[0] = g.astype(g_out.dtype)


def _rwkv_prep(prw, plora, mu, w0, w_up, a0, a_up, g_up, k_k, k_a, ts=256):
    b, s, w3 = prw.shape
    ts = min(ts, s)
    wl = plora.shape[-1]
    pad_rows = lambda m, lo: jnp.zeros((wl, RW_WIDTH), F32).at[lo:lo + m.shape[0]].set(m)
    wup = pad_rows(w_up, 0)
    aup = pad_rows(a_up, DECAY_LORA)
    gup = pad_rows(g_up, DECAY_LORA + AAA_LORA)
    mu_main = mu[:w3].reshape(1, w3)
    mu_lora = jnp.zeros((1, wl), F32).at[0, :RW_LORA].set(mu[w3:])
    row = lambda t: t.reshape(1, RW_WIDTH)
    blk = lambda w: pl.BlockSpec((1, ts, w), lambda bi, i: (bi, i, 0))
    prev = lambda w: pl.BlockSpec((1, 8, w), lambda bi, i: (bi, jnp.maximum(i * (ts // 8) - 1, 0), 0))
    full = lambda r, c: pl.BlockSpec((r, c), lambda bi, i: (0, 0))
    out_sds = lambda dt: jax.ShapeDtypeStruct((b, s, RW_WIDTH), dt)
    return pl.pallas_call(
        functools.partial(_rwkv_prep_body, ts=ts),
        grid=(b, s // ts),
        in_specs=[blk(w3), prev(w3), blk(wl), prev(wl), full(1, w3), full(1, wl),
                  full(1, RW_WIDTH), full(1, RW_WIDTH), full(1, RW_WIDTH), full(1, RW_WIDTH),
                  full(wl, RW_WIDTH), full(wl, RW_WIDTH), full(wl, RW_WIDTH)],
        out_specs=[blk(RW_WIDTH)] * 7,
        out_shape=[out_sds(F32 if i == 3 else BF16) for i in range(7)],
        compiler_params=_params("parallel", "parallel"),
        name="rwkv_prep",
    )(prw, prw, plora, plora, mu_main, mu_lora, row(w0), row(a0), row(k_k), row(k_a), wup, aup, gup)


def _cumsum_rows(ones_lower_bf16, x):
    hi = x.astype(BF16)
    rem = x - hi.astype(F32)
    mid = rem.astype(BF16)
    lo = (rem - mid.astype(F32)).astype(BF16)
    width = x.shape[1]
    out = jnp.dot(ones_lower_bf16, jnp.concatenate([hi, mid, lo], axis=1), preferred_element_type=F32)
    return out[:, :width] + out[:, width:2 * width] + out[:, 2 * width:]


def _rwkv_chunk_body(r_ref, k_ref, v_ref, lw_ref, kk_ref, a_ref, g_ref, rk_ref, lnw_ref, lnb_ref,
                     o_ref, st_ref, *, chunk, n_chunks, n_pairs, group):
    @pl.when(pl.program_id(1) == 0)
    def _():
        st_ref[...] = jnp.zeros_like(st_ref)

    c = chunk
    head = RW_HEAD
    lane = lax.broadcasted_iota(jnp.int32, (1, LANES), 1)
    head0 = lane < head
    row_c = lax.broadcasted_iota(jnp.int32, (c, LANES), 0)
    col_c = lax.broadcasted_iota(jnp.int32, (c, LANES), 1)
    col_in_head = jnp.where(col_c < head, col_c, col_c - head)
    strict = col_in_head < row_c
    lower = col_in_head <= row_c
    row_p = lax.broadcasted_iota(jnp.int32, (LANES, LANES), 0)
    col_p = lax.broadcasted_iota(jnp.int32, (LANES, LANES), 1)
    eye = jnp.where(row_p == col_p, 1.0, 0.0).astype(F32)
    same_head = (row_p < head) == (col_p < head)
    ones_lower = jnp.where(lax.broadcasted_iota(jnp.int32, (c, c), 1) <= lax.broadcasted_iota(jnp.int32, (c, c), 0),
                           1.0, 0.0).astype(BF16)
    zeros_c = jnp.zeros((c, LANES), BF16)
    n_doublings = int(math.log2(c)) - 1

    def only0(x):
        return jnp.where(head0, x, jnp.zeros_like(x))

    def only1(x):
        return jnp.where(head0, jnp.zeros_like(x), x)

    def head_sum(x):
        s0 = jnp.sum(only0(x), axis=-1, keepdims=True)
        s1 = jnp.sum(only1(x), axis=-1, keepdims=True)
        return jnp.where(head0, s0, s1)

    def each(fn, *lists):
        return [fn(*args) for args in zip(*lists)]

    def chunk_group(rows, pairs):
        lanes = [slice(p * LANES, (p + 1) * LANES) for p in pairs]
        load = lambda ref: [ref[0, rows, ln].astype(F32) for ln in lanes]
        r, k, v, lw, kkr, ag = load(r_ref), load(k_ref), load(v_ref), load(lw_ref), load(kk_ref), load(a_ref)
        kk = each(lambda t: t / jnp.maximum(jnp.sqrt(head_sum(t * t)), 1e-12), kkr)
        bm = each(lambda a, b: a * b, kk, ag)
        cw = each(lambda t: _cumsum_rows(ones_lower, t), lw)
        w_inv = each(lambda t: jnp.exp(-t), cw)
        b_t = each(lambda a, b: (a * b).astype(BF16), bm, w_inv)
        k_t = each(lambda a, b: (a * b).astype(BF16), k, w_inv)
        lhs = each(lambda kk_, cw_, lw_, r_: jnp.concatenate([-kk_ * jnp.exp(cw_ - lw_), r_ * jnp.exp(cw_)],
                                                             axis=0).astype(BF16), kk, cw, lw, r)
        g0 = each(lambda l, b, k_: lax.dot_general(only0(l), jnp.concatenate([b, k_], axis=0), _NT,
                                                   preferred_element_type=F32), lhs, b_t, k_t)
        g1 = each(lambda l, b, k_: lax.dot_general(only1(l), jnp.concatenate([k_, b], axis=0), _NT,
                                                   preferred_element_type=F32), lhs, b_t, k_t)
        top0 = each(lambda g_: jnp.where(strict, g_[:c], 0.0), g0)
        top1 = each(lambda g_: jnp.where(strict, g_[:c], 0.0), g1)
        bot = each(lambda a, b: jnp.concatenate([jnp.where(lower, a[c:], 0.0), jnp.where(lower, b[c:], 0.0)],
                                                axis=1).astype(BF16), g0, g1)
        power = each(lambda a, b: jnp.concatenate([only0(a), only1(b)], axis=0), top0, top1)
        inv = each(lambda n: eye + n, power)
        power = each(lambda t: jnp.dot(t.astype(BF16), t.astype(BF16), preferred_element_type=F32), power)
        for d in range(n_doublings):
            if d + 1 < n_doublings:
                both = each(lambda t, pw: jnp.dot(jnp.concatenate([t, pw], axis=0).astype(BF16), pw.astype(BF16),
                                                  preferred_element_type=F32), inv, power)
                inv = each(lambda t, bt: t + bt[:LANES], inv, both)
                power = each(lambda bt: bt[LANES:], both)
            else:
                inv = each(lambda t, pw: t + jnp.dot(t.astype(BF16), pw.astype(BF16), preferred_element_type=F32),
                           inv, power)
        st = [st_ref[p] for p in pairs]
        from_state = each(lambda l, s_: lax.dot_general(l, s_.astype(BF16), _NT, preferred_element_type=F32),
                          lhs, st)
        vb = each(lambda t: t.astype(BF16), v)
        x = each(lambda fs, a, b, v_: fs[:c] + jnp.dot(
            jnp.concatenate([a, b], axis=1).astype(BF16),
            jnp.concatenate([zeros_c, only0(v_), only1(v_), zeros_c], axis=0), preferred_element_type=F32),
            from_state, top0, top1, vb)
        tu = each(lambda t, x_: jnp.dot(t.astype(BF16), jnp.concatenate([only0(x_), only1(x_)], axis=0).astype(BF16),
                                        preferred_element_type=F32), inv, x)
        ub = each(lambda t: (t[:c] + t[c:]).astype(BF16), tu)
        y = each(lambda fs, b, u_, v_: fs[c:] + jnp.dot(
            b, jnp.concatenate([only0(u_), only0(v_), only1(v_), only1(u_)], axis=0), preferred_element_type=F32),
            from_state, bot, ub, vb)
        w_end = each(lambda t: t[c - 1:c], cw)
        tail = each(lambda e, t: jnp.exp(e - t), w_end, cw)
        upd = each(lambda u_, v_, bm_, k_, tl: lax.dot_general(
            jnp.concatenate([u_, v_], axis=0),
            jnp.concatenate([(bm_ * tl).astype(BF16), (k_ * tl).astype(BF16)], axis=0), _TN,
            preferred_element_type=F32), ub, vb, bm, k, tail)
        for p, s_, e, up in zip(pairs, st, w_end, upd):
            st_ref[p] = s_ * jnp.exp(e) + jnp.where(same_head, up, 0.0)
        for ln, y_, r_, k_, v_ in zip(lanes, y, r, k, v):
            yc = y_ - head_sum(y_) * (1.0 / head)
            var = head_sum(yc * yc) * (1.0 / head)
            yn = yc * lax.rsqrt(var + RW_GN_EPS) * lnw_ref[:, ln] + lnb_ref[:, ln]
            bonus = head_sum(r_ * k_ * rk_ref[:, ln]) * v_
            o_ref[rows, ln] = ((yn + bonus) * g_ref[0, rows, ln].astype(F32)).astype(o_ref.dtype)

    for ci in range(n_chunks):
        for p0 in range(0, n_pairs, group):
            chunk_group(slice(ci * c, (ci + 1) * c), list(range(p0, min(p0 + group, n_pairs))))


def _rwkv_chunked(r, k, v, lw, kk, a, g, r_k, ln_w, ln_b, chunk=RW_CHUNK, block=2 * RW_CHUNK, group=8):
    b, s, w = r.shape
    chunk = min(chunk, s)
    block = min(block, s)
    nb = s // block
    n_pairs = w // LANES
    blk = pl.BlockSpec((1, block, w), lambda bi, i: (bi, i, 0))
    par = pl.BlockSpec((1, w), lambda bi, i: (0, 0))
    return pl.pallas_call(
        functools.partial(_rwkv_chunk_body, chunk=chunk, n_chunks=block // chunk, n_pairs=n_pairs, group=group),
        grid=(b, nb),
        in_specs=[blk] * 7 + [par] * 3,
        out_specs=pl.BlockSpec((block, w), lambda bi, i: (bi * nb + i, 0)),
        out_shape=jax.ShapeDtypeStruct((b * s, w), BF16),
        scratch_shapes=[pltpu.VMEM((n_pairs, LANES, LANES), F32)],
        compiler_params=_params("parallel", "arbitrary"),
        name="rwkv_chunk",
    )(r, k, v, lw, kk, a, g, r_k.reshape(1, w), ln_w.reshape(1, w), ln_b.reshape(1, w))


def _hgrn_body(lbraw_ref, ng_ref, q_ref, f_ref, i_ref, g_ref, o_ref, st_ref, *, block, layer, group):
    @pl.when(pl.program_id(2) == 0)
    def _():
        st_ref[...] = jnp.zeros_like(st_ref)

    lbraw = lbraw_ref[...]
    e = jnp.exp(lbraw - jnp.max(lbraw, axis=0, keepdims=True))
    soft = e / jnp.sum(e, axis=0, keepdims=True)
    lb_all = jnp.sum(soft[:layer + 1], axis=0, keepdims=True) - soft[0:1]
    ng = ng_ref[...]

    c, sub = HG_CHUNK, HG_SUB
    n_sub = c // sub
    row = lax.broadcasted_iota(jnp.int32, (c, c), 0)
    col = lax.broadcasted_iota(jnp.int32, (c, c), 1)
    ones_lower = jnp.where(col <= row, 1.0, 0.0).astype(BF16)
    ones_sq = jnp.ones((LANES, LANES), BF16)
    ri = lax.broadcasted_iota(jnp.int32, (c, 1), 0)
    rs = lax.broadcasted_iota(jnp.int32, (sub, 1), 0)
    zeros_sub = jnp.zeros((sub, LANES), F32)
    off_row = lax.broadcasted_iota(jnp.int32, (c, (n_sub - 1) * c), 0) // sub
    off_col = lax.broadcasted_iota(jnp.int32, (c, (n_sub - 1) * c), 1) // c
    off_keep = off_row == off_col + 1
    lanes = [slice(h * LANES, (h + 1) * LANES) for h in range(group)]
    lb = [lb_all[:, ln] for ln in lanes]

    def each(fn, *lists):
        return [fn(*args) for args in zip(*lists)]

    def diag_pieces(q, k, b):
        pieces = []
        for i in range(n_sub):
            sl = slice(sub * i, sub * (i + 1))
            qi, ki, bi = q[sl], k[sl], b[sl]
            for s in range(sub):
                decay = jnp.exp2(jnp.where(rs >= s, bi - bi[s:s + 1], NEG_BIG))
                pieces.append(qi * decay * ki[s:s + 1])
        return jnp.concatenate(pieces, axis=0)

    def diag_apply(sums, iv):
        parts = []
        for i in range(n_sub):
            od = zeros_sub
            for s in range(sub):
                r0 = (i * sub + s) * sub
                od = od + sums[r0:r0 + sub] * iv[sub * i + s:sub * i + s + 1]
            parts.append(od)
        return jnp.concatenate(parts, axis=0)

    def chunk_step(ci, carry):
        rows = pl.ds(pl.multiple_of(ci * c, c), c)
        q = [q_ref[rows, ln] for ln in lanes]
        iv = [i_ref[rows, ln] for ln in lanes]
        fg = each(lambda ln, lb_: lb_ + (1.0 - lb_) * _sigmoid(f_ref[rows, ln]), lanes, lb)
        k = each(lambda t: 1.0 - t, fg)
        b = each(lambda t: _cumsum_rows(ones_lower, jnp.log(t)) * LOG2_E, fg)
        st = [st_ref[h] for h in range(group)]
        o_inter = each(lambda q_, b_, s_: _dot_bf16(q_ * jnp.exp2(b_), s_, _NT), q, b, st)

        def off_scores(q_, k_, b_):
            brefs = [b_[sub * i - 1:sub * i] for i in range(1, n_sub)]
            bref_rows = jnp.concatenate([zeros_sub] + [jnp.broadcast_to(t, (sub, LANES)) for t in brefs], axis=0)
            qh = q_ * jnp.exp2(jnp.where(ri >= sub, b_ - bref_rows, NEG_BIG))
            kh = jnp.concatenate([k_ * jnp.exp2(jnp.where(ri < sub * (i + 1), t - b_, NEG_BIG))
                                  for i, t in enumerate(brefs)], axis=0)
            return _dot_bf16(qh, kh, _NT)
        scores = each(off_scores, q, k, b)

        stacked = each(diag_pieces, q, k, b)
        sums = each(lambda t: jnp.dot(t.astype(BF16), ones_sq, preferred_element_type=F32), stacked)
        ivb = each(lambda t: t.astype(BF16), iv)
        o_off = each(lambda sc, v_: jnp.dot(jnp.where(off_keep, sc, 0.0).astype(BF16),
                                            jnp.concatenate([v_] * (n_sub - 1), axis=0),
                                            preferred_element_type=F32), scores, ivb)
        b_end = each(lambda t: t[c - 1:c], b)
        upd = each(lambda v_, k_, b_, e_: lax.dot_general(v_, (k_ * jnp.exp2(e_ - b_)).astype(BF16), _TN,
                                                          preferred_element_type=F32), ivb, k, b, b_end)
        for h in range(group):
            st_ref[h] = st[h] * jnp.exp2(b_end[h]) + upd[h]
            o = o_inter[h] + o_off[h] + diag_apply(sums[h], iv[h])
            o = o * lax.rsqrt(jnp.mean(o * o, axis=-1, keepdims=True) + EPS) * ng
            gg = g_ref[rows, lanes[h]]
            o_ref[rows, lanes[h]] = (o * (gg * _sigmoid(gg))).astype(o_ref.dtype)
        return carry

    lax.fori_loop(0, block // c, chunk_step, 0, unroll=8)


def _hgrn2(p, lb_raw, norm_g, batch, seq, layer, block=1024, group=4):
    block = min(block, seq)
    nb = seq // block
    depth = lb_raw.shape[0]
    n_groups = HG_HEADS // group
    width = group * LANES
    col = lambda part: pl.BlockSpec((block, width), lambda b, h, i: (b * nb + i, part * n_groups + h))
    return pl.pallas_call(
        functools.partial(_hgrn_body, block=block, layer=layer, group=group),
        grid=(batch, n_groups, nb),
        in_specs=[pl.BlockSpec((depth, width), lambda b, h, i: (0, h)),
                  pl.BlockSpec((1, LANES), lambda b, h, i: (0, 0)),
                  col(0), col(1), col(2), col(3)],
        out_specs=pl.BlockSpec((block, width), lambda b, h, i: (b * nb + i, h)),
        out_shape=jax.ShapeDtypeStruct((batch * seq, D_MODEL), BF16),
        scratch_shapes=[pltpu.VMEM((group, LANES, LANES), F32)],
        compiler_params=_params("parallel", "parallel", "arbitrary"),
        name="hgrn2",
    )(lb_raw, norm_g.reshape(1, LANES), p, p, p, p)


def _ffn(x, h, layer, w_gate, w_up, w_down, next_g, last):
    hidden = _ffn_up(h, w_gate, w_up, layer)
    return _matmul_residual_norm([hidden], w_down, layer, x, next_g, F32 if last else BF16, emit_x=not last, tm=256)


def _even_mixer(x, h, batch, seq, layer, j, w_in, w_out, ffn_g, lam_vec, subln_g, mu, w0, w_up, a0, a_up, g_up,
                k_k, k_a, r_k, ln_w, ln_b):
    qk = _qk_proj(h, w_in, j, seq)
    v = _matmul(h, w_in, j, 2 * DA_WIDTH, DA_WIDTH, BF16, tm=2048)
    lam_init = 0.8 - 0.6 * math.exp(-0.3 * layer)
    a_out = _diff_attention(qk, v, lam_vec, subln_g, batch, seq, lam_init)

    rw0 = 3 * DA_WIDTH
    prw = _matmul(h, w_in, j, rw0, 3 * RW_WIDTH, F32, tm=2048)
    w_lora = jnp.pad(w_in[j, :, rw0 + 3 * RW_WIDTH:], ((0, 0), (0, RW_LORA_PAD - RW_LORA)))
    plora = _matmul(h, w_lora[None], 0, 0, RW_LORA_PAD, F32)
    tok = _rwkv_prep(prw.reshape(batch, seq, -1), plora.reshape(batch, seq, -1), mu, w0, w_up, a0, a_up, g_up,
                     k_k, k_a)
    b_out = _rwkv_chunked(*tok, r_k, ln_w, ln_b)
    return _matmul_residual_norm([a_out, b_out], w_out, j, x, ffn_g, BF16)


def _odd_mixer(x, h, batch, seq, layer, j, w_in, w_out, ffn_g, lb_raw, hg_norm_g):
    p = _matmul(h, w_in, j, 0, w_in.shape[-1], F32, tm=2048)
    o = _hgrn2(p, lb_raw, hg_norm_g, batch, seq, layer)
    return _matmul_residual_norm([o], w_out, j, x, ffn_g, BF16)


def kernel(x, attn_norm_g, ffn_norm_g, final_norm_g, even_w_in, even_w_out, da_lambda, da_subln_g, rw_mu, rw_w0, rw_w_up, rw_a0, rw_a_up, rw_g_up, rw_k_k, rw_k_a, rw_r_k, rw_ln_w, rw_ln_b, odd_w_in, odd_w_out, hg_lower_bound, hg_norm_g, ffn_w_gate, ffn_w_up, ffn_w_down):
    batch, seq, d = x.shape
    depth = attn_norm_g.shape[0]
    xs = x.reshape(batch * seq, d)
    w_down, w_out_even, w_out_odd = (t.astype(BF16) for t in (ffn_w_down, even_w_out, odd_w_out))
    h = _rmsnorm(xs, attn_norm_g[0], BF16)
    for layer in range(depth):
        j = layer // 2
        if layer % 2 == 0:
            xs, h = _even_mixer(xs, h, batch, seq, layer, j, even_w_in, w_out_even, ffn_norm_g[layer], da_lambda[j],
                                da_subln_g[j], rw_mu[j], rw_w0[j], rw_w_up[j], rw_a0[j], rw_a_up[j], rw_g_up[j],
                                rw_k_k[j], rw_k_a[j], rw_r_k[j], rw_ln_w[j], rw_ln_b[j])
        else:
            xs, h = _odd_mixer(xs, h, batch, seq, layer, j, odd_w_in, w_out_odd, ffn_norm_g[layer],
                               hg_lower_bound, hg_norm_g[j])
        last = layer == depth - 1
        next_g = final_norm_g if last else attn_norm_g[layer + 1]
        xs, h = _ffn(xs, h, layer, ffn_w_gate, ffn_w_up, w_down, next_g, last)
    return h.reshape(batch, seq, d)
```

```python
import functools
import math

import jax
import jax.numpy as jnp
import numpy as np
from jax import lax
from jax.experimental import pallas as pl
from jax.experimental.pallas import tpu as pltpu

D_MODEL = 2048
EPS = 1e-6

DA_WIDTH = D_MODEL // 2
DA_HEADS = 8
DA_HEAD_DIM = DA_WIDTH // DA_HEADS // 2
DA_V_DIM = 2 * DA_HEAD_DIM
ROPE_DIM = DA_HEAD_DIM // 4
ROPE_THETA = 500000.0

RW_WIDTH = D_MODEL // 2
RW_HEAD = 64
RW_HEADS = RW_WIDTH // RW_HEAD
DECAY_LORA = 64
AAA_LORA = 64
GATE_LORA = 32
RW_LORA = DECAY_LORA + AAA_LORA + GATE_LORA
RW_LORA_PAD = 256
RW_GN_EPS = 64e-5
RW_CHUNK = 64

HG_EXPAND = 128
HG_HEADS = D_MODEL // HG_EXPAND
HG_CHUNK = 64
HG_SUB = 8

LANES = 128
NEG_BIG = -1e30
LOG2_E = 1.0 / math.log(2.0)
VMEM_LIMIT = 56 * 1024 * 1024

F32 = jnp.float32
BF16 = jnp.bfloat16

_NN = (((1,), (0,)), ((), ()))
_NT = (((1,), (1,)), ((), ()))
_TN = (((0,), (0,)), ((), ()))


def _dot_bf16(a, b, dims=_NN):
    return lax.dot_general(a.astype(BF16), b.astype(BF16), dims, preferred_element_type=F32)


def _sigmoid(x):
    return 1.0 / (1.0 + jnp.exp(-x))


def _params(*sem):
    return pltpu.CompilerParams(dimension_semantics=sem, vmem_limit_bytes=VMEM_LIMIT)


def _rmsnorm_body(x_ref, g_ref, o_ref):
    x = x_ref[...]
    o_ref[...] = (x * lax.rsqrt(jnp.mean(x * x, axis=-1, keepdims=True) + EPS) * g_ref[...]).astype(o_ref.dtype)


def _rmsnorm(x2d, g, out_dtype, tm=512):
    t, d = x2d.shape
    tm = min(tm, t)
    return pl.pallas_call(
        _rmsnorm_body,
        grid=(t // tm,),
        in_specs=[pl.BlockSpec((tm, d), lambda i: (i, 0)), pl.BlockSpec((1, d), lambda i: (0, 0))],
        out_specs=pl.BlockSpec((tm, d), lambda i: (i, 0)),
        out_shape=jax.ShapeDtypeStruct((t, d), out_dtype),
        compiler_params=_params("parallel"),
        name="rmsnorm",
    )(x2d, g.reshape(1, d))


def _mm_body(a_ref, w_ref, o_ref):
    o_ref[...] = jnp.dot(a_ref[...], w_ref[...].astype(BF16), preferred_element_type=F32).astype(o_ref.dtype)


def _matmul(a, w, layer, col0, n, out_dtype, tm=1024, tn=512):
    m, k = a.shape
    tm, tn = min(tm, m), min(tn, n)
    c0 = col0 // tn
    return pl.pallas_call(
        _mm_body,
        grid=(m // tm, n // tn),
        in_specs=[pl.BlockSpec((tm, k), lambda i, j: (i, 0)),
                  pl.BlockSpec((None, k, tn), lambda i, j: (layer, 0, c0 + j))],
        out_specs=pl.BlockSpec((tm, tn), lambda i, j: (i, j)),
        out_shape=jax.ShapeDtypeStruct((m, n), out_dtype),
        compiler_params=_params("parallel", "arbitrary"),
        name="matmul",
    )(a, w)


def _norm_rows(x, g):
    return x * lax.rsqrt(jnp.mean(x * x, axis=-1, keepdims=True) + EPS) * g


def _res_norm_body(*refs, n_lhs, emit_x):
    a_refs, w_refs = refs[:n_lhs], refs[n_lhs:2 * n_lhs]
    r_ref, g_ref = refs[2 * n_lhs:2 * n_lhs + 2]
    outs = refs[2 * n_lhs + 2:]
    x = r_ref[...]
    for a_ref, w_ref in zip(a_refs, w_refs):
        x = x + jnp.dot(a_ref[...], w_ref[...], preferred_element_type=F32)
    if emit_x:
        outs[0][...] = x
    outs[-1][...] = _norm_rows(x, g_ref[...]).astype(outs[-1].dtype)


def _matmul_residual_norm(lhs, w, layer, res, g, norm_dtype, emit_x=True, tm=512):
    m, n = res.shape
    k = lhs[0].shape[1]
    tm = min(tm, m)
    row = lambda width: pl.BlockSpec((tm, width), lambda i: (i, 0))
    w_specs = [pl.BlockSpec((None, k, n), lambda i, p=p: (layer, p, 0), pipeline_mode=pl.Buffered(1))
               for p in range(len(lhs))]
    out_specs = ([row(n)] if emit_x else []) + [row(n)]
    out_shape = ([jax.ShapeDtypeStruct((m, n), F32)] if emit_x else []) + [jax.ShapeDtypeStruct((m, n), norm_dtype)]
    outs = pl.pallas_call(
        functools.partial(_res_norm_body, n_lhs=len(lhs), emit_x=emit_x),
        grid=(m // tm,),
        in_specs=[row(k)] * len(lhs) + w_specs + [row(n), pl.BlockSpec((1, n), lambda i: (0, 0))],
        out_specs=out_specs,
        out_shape=out_shape,
        compiler_params=_params("parallel"),
        name="matmul_residual_norm",
    )(*lhs, *([w] * len(lhs)), res, g.reshape(1, n))
    return outs if emit_x else (None, outs[0])


def _ffn_up_body(h_ref, wg_ref, wu_ref, o_ref):
    h = h_ref[...]
    gate = jnp.dot(h, wg_ref[...].astype(BF16), preferred_element_type=F32)
    up = jnp.dot(h, wu_ref[...].astype(BF16), preferred_element_type=F32)
    o_ref[...] = (gate * _sigmoid(gate) * up).astype(o_ref.dtype)


def _ffn_up(h, wg, wu, layer, tm=2048, tn=512):
    m, k = h.shape
    n = wg.shape[-1]
    tm, tn = min(tm, m), min(tn, n)
    wspec = pl.BlockSpec((None, k, tn), lambda i, j: (layer, 0, j))
    return pl.pallas_call(
        _ffn_up_body,
        grid=(m // tm, n // tn),
        in_specs=[pl.BlockSpec((tm, k), lambda i, j: (i, 0)), wspec, wspec],
        out_specs=pl.BlockSpec((tm, tn), lambda i, j: (i, j)),
        out_shape=jax.ShapeDtypeStruct((m, n), BF16),
        compiler_params=_params("parallel", "arbitrary"),
        name="ffn_up",
    )(h, wg, wu)


def _qk_rope_body(h_ref, w_ref, cos_ref, sin_lo_ref, sin_hi_ref, o_ref, *, n_q_tiles, tn, q_scale, row_parts):
    w = w_ref[...].astype(BF16)
    scale = jnp.where(pl.program_id(1) < n_q_tiles, q_scale, 1.0).astype(F32)
    half = ROPE_DIM // 2
    tm = h_ref.shape[0]
    rows_per_part = tm // row_parts
    accs = [jnp.dot(h_ref[r0:r0 + rows_per_part, :], w, preferred_element_type=F32)
            for r0 in range(0, tm, rows_per_part)]
    for p, acc in enumerate(accs):
        rows = slice(p * rows_per_part, (p + 1) * rows_per_part)
        cos = cos_ref[rows, :]
        sin_lo = sin_lo_ref[rows, :]
        sin_hi = sin_hi_ref[rows, :]
        for g in range(tn // LANES):
            x = acc[:, g * LANES:(g + 1) * LANES]
            y = x * cos + pltpu.roll(x, half, 1) * sin_hi + pltpu.roll(x, LANES - half, 1) * sin_lo
            o_ref[rows, g * LANES:(g + 1) * LANES] = (y * scale).astype(o_ref.dtype)


def _rope_tables(seq):
    pos = np.arange(seq, dtype=np.float64)
    inv_freq = 1.0 / (ROPE_THETA ** (np.arange(0, ROPE_DIM, 2, dtype=np.float64) / ROPE_DIM))
    ang = pos[:, None] * inv_freq[None, :]
    cos, sin = np.cos(ang), np.sin(ang)
    half = ROPE_DIM // 2
    pad = np.zeros((seq, DA_HEAD_DIM - ROPE_DIM))
    zer = np.zeros((seq, half))
    cos_c = np.concatenate([cos, cos, pad + 1.0], axis=1)
    sin_lo = np.concatenate([-sin, zer, pad], axis=1)
    sin_hi = np.concatenate([zer, sin, pad], axis=1)
    rep = lambda t: jnp.asarray(np.concatenate([t, t], axis=1), dtype=F32)
    return rep(cos_c), rep(sin_lo), rep(sin_hi)


def _qk_proj(h, w_in, layer, seq, tm=2048, tn=512):
    m, k = h.shape
    n = 2 * DA_WIDTH
    tm, tn = min(tm, seq), min(tn, n)
    cos, sin_lo, sin_hi = _rope_tables(seq)
    s_tiles = seq // tm
    body = functools.partial(_qk_rope_body, n_q_tiles=DA_WIDTH // tn, tn=tn, q_scale=DA_HEAD_DIM ** -0.5 * LOG2_E,
                             row_parts=16)
    tab = pl.BlockSpec((tm, LANES), lambda i, j: (i % s_tiles, 0))
    return pl.pallas_call(
        body,
        grid=(m // tm, n // tn),
        in_specs=[pl.BlockSpec((tm, k), lambda i, j: (i, 0)),
                  pl.BlockSpec((None, k, tn), lambda i, j: (layer, 0, j)), tab, tab, tab],
        out_specs=pl.BlockSpec((tm, tn), lambda i, j: (i, j)),
        out_shape=jax.ShapeDtypeStruct((m, n), BF16),
        compiler_params=_params("parallel", "arbitrary"),
        name="qk_proj_rope",
    )(h, w_in, cos, sin_lo, sin_hi)


def _attn_body(lam_ref, g_ref, q_ref, k_ref, v_ref, o_ref, *, tq, tk, lam_init, group, unroll, lookahead):
    qi = pl.program_id(2)
    lane = lax.broadcasted_iota(jnp.int32, (1, LANES), 1)
    chains = []
    for h in range(group):
        ln = slice(h * LANES, (h + 1) * LANES)
        q = q_ref[:, ln]
        zero = jnp.zeros_like(q)
        chains.append((ln, jnp.where(lane < DA_HEAD_DIM, q, zero)))
        chains.append((ln, jnp.where(lane >= DA_HEAD_DIM, q, zero)))

    def score(c, rows, keep=None):
        ln, q = chains[c]
        s = lax.dot_general(k_ref[rows, ln], q, _NT, preferred_element_type=F32)
        if keep is not None:
            s = jnp.where(keep, s, NEG_BIG)
        return s, jnp.max(s, axis=0, keepdims=True)

    def run_blocks(blocks, carry):
        n = len(chains)
        stages = [(rows, keep, c) for rows, keep in blocks for c in range(n)]
        carry = list(carry)
        pending = None

        def apply(update):
            c, m_new, alpha, p_sum, pv = update
            _, l, acc = carry[c]
            carry[c] = (m_new, alpha * l + p_sum, alpha * acc + pv)

        issue = lambda st: score(st[2], st[0], st[1])
        queue = [issue(st) for st in stages[:lookahead]]
        for idx, (rows, _, c) in enumerate(stages):
            if idx + lookahead < len(stages):
                queue.append(issue(stages[idx + lookahead]))
            s_cur = queue.pop(0)
            m = carry[c][0]
            m_new = jnp.maximum(m, s_cur[1])
            p = jnp.exp2(s_cur[0] - m_new)
            pv = lax.dot_general(v_ref[rows, chains[c][0]], p.astype(BF16), _TN, preferred_element_type=F32)
            if pending is not None:
                apply(pending)
            pending = (c, m_new, jnp.exp2(m - m_new), jnp.sum(p, axis=0, keepdims=True), pv)
        apply(pending)
        return tuple(carry)

    def block(j):
        return pl.ds(pl.multiple_of(j * tk, tk), tk)

    init = tuple((jnp.full((1, tq), NEG_BIG, F32), jnp.zeros((1, tq), F32), jnp.zeros((LANES, tq), F32))
                 for _ in chains)
    per_q = tq // tk
    n_full = qi * per_q

    carry = lax.fori_loop(0, n_full // unroll,
                          lambda t, cr: run_blocks([(block(unroll * t + u), None) for u in range(unroll)], cr), init)
    done = n_full // unroll * unroll
    carry = lax.fori_loop(0, (n_full - done) // 2,
                          lambda t, cr: run_blocks([(block(done + 2 * t + u), None) for u in range(2)], cr), carry)
    carry = lax.fori_loop(done + (n_full - done) // 2 * 2, n_full, lambda j, cr: run_blocks([(block(j), None)], cr),
                          carry)
    diagonal = []
    for d in range(per_q):
        key = lax.broadcasted_iota(jnp.int32, (tk, tq), 0) + d * tk
        diagonal.append((block(n_full + d), key <= lax.broadcasted_iota(jnp.int32, (tk, tq), 1)))
    carry = run_blocks(diagonal, carry)

    lf = lam_ref[...]
    lam = (jnp.exp(jnp.sum(lf[0:1] * lf[1:2], axis=-1, keepdims=True))
           - jnp.exp(jnp.sum(lf[2:3] * lf[3:4], axis=-1, keepdims=True)) + lam_init)
    for h in range(group):
        (ln, _), (_, l0, acc0), (_, l1, acc1) = chains[2 * h], carry[2 * h], carry[2 * h + 1]
        o = acc0 / l0 - lam * (acc1 / l1)
        o = o * lax.rsqrt(jnp.mean(o * o, axis=0, keepdims=True) + EPS)
        o_ref[:, ln] = (o.T * g_ref[...] * (1.0 - lam_init)).astype(o_ref.dtype)


def _diff_attention(qk, v, lam_vec, subln_g, batch, seq, lam_init, tq=256, tk=256, group=4, unroll=4, lookahead=8):
    tq = min(tq, seq)
    tk = min(tk, tq)
    nq = seq // tq
    n_groups = DA_HEADS // group
    width = group * LANES
    body = functools.partial(_attn_body, tq=tq, tk=tk, lam_init=lam_init, group=group, unroll=unroll,
                             lookahead=lookahead)
    return pl.pallas_call(
        body,
        grid=(batch, n_groups, nq),
        in_specs=[
            pl.BlockSpec((4, DA_HEAD_DIM), lambda b, h, i: (0, 0)),
            pl.BlockSpec((1, DA_V_DIM), lambda b, h, i: (0, 0)),
            pl.BlockSpec((tq, width), lambda b, h, i: (b * nq + i, h)),
            pl.BlockSpec((seq, width), lambda b, h, i: (b, n_groups + h)),
            pl.BlockSpec((seq, width), lambda b, h, i: (b, h)),
        ],
        out_specs=pl.BlockSpec((tq, width), lambda b, h, i: (b * nq + i, h)),
        out_shape=jax.ShapeDtypeStruct((batch * seq, DA_WIDTH), BF16),
        compiler_params=_params("parallel", "parallel", "arbitrary"),
        name="diff_attention",
    )(lam_vec, subln_g.reshape(1, DA_V_DIM), qk, qk, v)


def _rwkv_prep_body(x_ref, xprev_ref, l_ref, lprev_ref, mu_ref, mul_ref, w0_ref, a0_ref, kk_ref, ka_ref,
                    wup_ref, aup_ref, gup_ref,
                    r_out, k_out, v_out, lw_out, kk_out, a_out, g_out, *, ts):
    first = pl.program_id(1) == 0
    row = lax.broadcasted_iota(jnp.int32, (ts, 1), 0)

    def token_shift(cur_ref, prev_ref, mu):
        cur = cur_ref[0]
        last = jnp.where(first, 0.0, prev_ref[0][7:8, :])
        prev = jnp.where(row == 0, last, pltpu.roll(cur, 1, 0))
        return cur + (prev - cur) * mu

    xm = token_shift(x_ref, xprev_ref, mu_ref[...])
    xl = token_shift(l_ref, lprev_ref, mul_ref[...])
    r = xm[:, :RW_WIDTH]
    k = xm[:, RW_WIDTH:2 * RW_WIDTH]
    v = xm[:, 2 * RW_WIDTH:]
    z = -(w0_ref[...] + _dot_bf16(jnp.tanh(xl), wup_ref[...]))
    softplus = jnp.maximum(z, 0.0) + jnp.log(1.0 + jnp.exp(-jnp.abs(z)))
    w_log = -softplus - 0.5
    a = _sigmoid(a0_ref[...] + _dot_bf16(xl, aup_ref[...]))
    g = _dot_bf16(_sigmoid(xl), gup_ref[...])
    r_out[0] = r.astype(r_out.dtype)
    k_out[0] = (k * (1.0 + (a - 1.0) * ka_ref[...])).astype(k_out.dtype)
    v_out[0] = v.astype(v_out.dtype)
    lw_out[0] = -jnp.exp(w_log)
    kk_out[0] = (k * kk_ref[...]).astype(kk_out.dtype)
    a_out[0] = a.astype(a_out.dtype)
    g_out[0] = g.astype(g_out.dtype)


def _rwkv_prep(prw, plora, mu, w0, w_up, a0, a_up, g_up, k_k, k_a, ts=256):
    b, s, w3 = prw.shape
    ts = min(ts, s)
    wl = plora.shape[-1]
    pad_rows = lambda m, lo: jnp.zeros((wl, RW_WIDTH), F32).at[lo:lo + m.shape[0]].set(m)
    wup = pad_rows(w_up, 0)
    aup = pad_rows(a_up, DECAY_LORA)
    gup = pad_rows(g_up, DECAY_LORA + AAA_LORA)
    mu_main = mu[:w3].reshape(1, w3)
    mu_lora = jnp.zeros((1, wl), F32).at[0, :RW_LORA].set(mu[w3:])
    row = lambda t: t.reshape(1, RW_WIDTH)
    blk = lambda w: pl.BlockSpec((1, ts, w), lambda bi, i: (bi, i, 0))
    prev = lambda w: pl.BlockSpec((1, 8, w), lambda bi, i: (bi, jnp.maximum(i * (ts // 8) - 1, 0), 0))
    full = lambda r, c: pl.BlockSpec((r, c), lambda bi, i: (0, 0))
    out_sds = lambda dt: jax.ShapeDtypeStruct((b, s, RW_WIDTH), dt)
    return pl.pallas_call(
        functools.partial(_rwkv_prep_body, ts=ts),
        grid=(b, s // ts),
        in_specs=[blk(w3), prev(w3), blk(wl), prev(wl), full(1, w3), full(1, wl),
                  full(1, RW_WIDTH), full(1, RW_WIDTH), full(1, RW_WIDTH), full(1, RW_WIDTH),
                  full(wl, RW_WIDTH), full(wl, RW_WIDTH), full(wl, RW_WIDTH)],
        out_specs=[blk(RW_WIDTH)] * 7,
        out_shape=[out_sds(F32 if i == 3 else BF16) for i in range(7)],
        compiler_params=_params("parallel", "parallel"),
        name="rwkv_prep",
    )(prw, prw, plora, plora, mu_main, mu_lora, row(w0), row(a0), row(k_k), row(k_a), wup, aup, gup)


def _cumsum_rows(ones_lower_bf16, x):
    hi = x.astype(BF16)
    rem = x - hi.astype(F32)
    mid = rem.astype(BF16)
    lo = (rem - mid.astype(F32)).astype(BF16)
    width = x.shape[1]
    out = jnp.dot(ones_lower_bf16, jnp.concatenate([hi, mid, lo], axis=1), preferred_element_type=F32)
    return out[:, :width] + out[:, width:2 * width] + out[:, 2 * width:]


def _rwkv_chunk_body(r_ref, k_ref, v_ref, lw_ref, kk_ref, a_ref, g_ref, rk_ref, lnw_ref, lnb_ref,
                     o_ref, st_ref, *, chunk, n_chunks, n_pairs, group):
    @pl.when(pl.program_id(1) == 0)
    def _():
        st_ref[...] = jnp.zeros_like(st_ref)

    c = chunk
    head = RW_HEAD
    lane = lax.broadcasted_iota(jnp.int32, (1, LANES), 1)
    head0 = lane < head
    row_c = lax.broadcasted_iota(jnp.int32, (c, LANES), 0)
    col_c = lax.broadcasted_iota(jnp.int32, (c, LANES), 1)
    col_in_head = jnp.where(col_c < head, col_c, col_c - head)
    strict = col_in_head < row_c
    lower = col_in_head <= row_c
    row_p = lax.broadcasted_iota(jnp.int32, (LANES, LANES), 0)
    col_p = lax.broadcasted_iota(jnp.int32, (LANES, LANES), 1)
    eye = jnp.where(row_p == col_p, 1.0, 0.0).astype(F32)
    same_head = (row_p < head) == (col_p < head)
    ones_lower = jnp.where(lax.broadcasted_iota(jnp.int32, (c, c), 1) <= lax.broadcasted_iota(jnp.int32, (c, c), 0),
                           1.0, 0.0).astype(BF16)
    zeros_c = jnp.zeros((c, LANES), BF16)
    n_doublings = int(math.log2(c)) - 1

    def only0(x):
        return jnp.where(head0, x, jnp.zeros_like(x))

    def only1(x):
        return jnp.where(head0, jnp.zeros_like(x), x)

    def head_sum(x):
        s0 = jnp.sum(only0(x), axis=-1, keepdims=True)
        s1 = jnp.sum(only1(x), axis=-1, keepdims=True)
        return jnp.where(head0, s0, s1)

    def each(fn, *lists):
        return [fn(*args) for args in zip(*lists)]

    def chunk_group(rows, pairs):
        lanes = [slice(p * LANES, (p + 1) * LANES) for p in pairs]
        load = lambda ref: [ref[0, rows, ln].astype(F32) for ln in lanes]
        r, k, v, lw, kkr, ag = load(r_ref), load(k_ref), load(v_ref), load(lw_ref), load(kk_ref), load(a_ref)
        kk = each(lambda t: t / jnp.maximum(jnp.sqrt(head_sum(t * t)), 1e-12), kkr)
        bm = each(lambda a, b: a * b, kk, ag)
        cw = each(lambda t: _cumsum_rows(ones_lower, t), lw)
        w_inv = each(lambda t: jnp.exp(-t), cw)
        b_t = each(lambda a, b: (a * b).astype(BF16), bm, w_inv)
        k_t = each(lambda a, b: (a * b).astype(BF16), k, w_inv)
        lhs = each(lambda kk_, cw_, lw_, r_: jnp.concatenate([-kk_ * jnp.exp(cw_ - lw_), r_ * jnp.exp(cw_)],
                                                             axis=0).astype(BF16), kk, cw, lw, r)
        g0 = each(lambda l, b, k_: lax.dot_general(only0(l), jnp.concatenate([b, k_], axis=0), _NT,
                                                   preferred_element_type=F32), lhs, b_t, k_t)
        g1 = each(lambda l, b, k_: lax.dot_general(only1(l), jnp.concatenate([k_, b], axis=0), _NT,
                                                   preferred_element_type=F32), lhs, b_t, k_t)
        top0 = each(lambda g_: jnp.where(strict, g_[:c], 0.0), g0)
        top1 = each(lambda g_: jnp.where(strict, g_[:c], 0.0), g1)
        bot = each(lambda a, b: jnp.concatenate([jnp.where(lower, a[c:], 0.0), jnp.where(lower, b[c:], 0.0)],
                                                axis=1).astype(BF16), g0, g1)
        power = each(lambda a, b: jnp.concatenate([only0(a), only1(b)], axis=0), top0, top1)
        inv = each(lambda n: eye + n, power)
        power = each(lambda t: jnp.dot(t.astype(BF16), t.astype(BF16), preferred_element_type=F32), power)
        for d in range(n_doublings):
            if d + 1 < n_doublings:
                both = each(lambda t, pw: jnp.dot(jnp.concatenate([t, pw], axis=0).astype(BF16), pw.astype(BF16),
                                                  preferred_element_type=F32), inv, power)
                inv = each(lambda t, bt: t + bt[:LANES], inv, both)
                power = each(lambda bt: bt[LANES:], both)
            else:
                inv = each(lambda t, pw: t + jnp.dot(t.astype(BF16), pw.astype(BF16), preferred_element_type=F32),
                           inv, power)
        st = [st_ref[p] for p in pairs]
        from_state = each(lambda l, s_: lax.dot_general(l, s_.astype(BF16), _NT, preferred_element_type=F32),
                          lhs, st)
        vb = each(lambda t: t.astype(BF16), v)
        x = each(lambda fs, a, b, v_: fs[:c] + jnp.dot(
            jnp.concatenate([a, b], axis=1).astype(BF16),
            jnp.concatenate([zeros_c, only0(v_), only1(v_), zeros_c], axis=0), preferred_element_type=F32),
            from_state, top0, top1, vb)
        tu = each(lambda t, x_: jnp.dot(t.astype(BF16), jnp.concatenate([only0(x_), only1(x_)], axis=0).astype(BF16),
                                        preferred_element_type=F32), inv, x)
        ub = each(lambda t: (t[:c] + t[c:]).astype(BF16), tu)
        y = each(lambda fs, b, u_, v_: fs[c:] + jnp.dot(
            b, jnp.concatenate([only0(u_), only0(v_), only1(v_), only1(u_)], axis=0), preferred_element_type=F32),
            from_state, bot, ub, vb)
        w_end = each(lambda t: t[c - 1:c], cw)
        tail = each(lambda e, t: jnp.exp(e - t), w_end, cw)
        upd = each(lambda u_, v_, bm_, k_, tl: lax.dot_general(
            jnp.concatenate([u_, v_], axis=0),
            jnp.concatenate([(bm_ * tl).astype(BF16), (k_ * tl).astype(BF16)], axis=0), _TN,
            preferred_element_type=F32), ub, vb, bm, k, tail)
        for p, s_, e, up in zip(pairs, st, w_end, upd):
            st_ref[p] = s_ * jnp.exp(e) + jnp.where(same_head, up, 0.0)
        for ln, y_, r_, k_, v_ in zip(lanes, y, r, k, v):
            yc = y_ - head_sum(y_) * (1.0 / head)
            var = head_sum(yc * yc) * (1.0 / head)
            yn = yc * lax.rsqrt(var + RW_GN_EPS) * lnw_ref[:, ln] + lnb_ref[:, ln]
            bonus = head_sum(r_ * k_ * rk_ref[:, ln]) * v_
            o_ref[rows, ln] = ((yn + bonus) * g_ref[0, rows, ln].astype(F32)).astype(o_ref.dtype)

    for ci in range(n_chunks):
        for p0 in range(0, n_pairs, group):
            chunk_group(slice(ci * c, (ci + 1) * c), list(range(p0, min(p0 + group, n_pairs))))


def _rwkv_chunked(r, k, v, lw, kk, a, g, r_k, ln_w, ln_b, chunk=RW_CHUNK, block=2 * RW_CHUNK, group=8):
    b, s, w = r.shape
    chunk = min(chunk, s)
    block = min(block, s)
    nb = s // block
    n_pairs = w // LANES
    blk = pl.BlockSpec((1, block, w), lambda bi, i: (bi, i, 0))
    par = pl.BlockSpec((1, w), lambda bi, i: (0, 0))
    return pl.pallas_call(
        functools.partial(_rwkv_chunk_body, chunk=chunk, n_chunks=block // chunk, n_pairs=n_pairs, group=group),
        grid=(b, nb),
        in_specs=[blk] * 7 + [par] * 3,
        out_specs=pl.BlockSpec((block, w), lambda bi, i: (bi * nb + i, 0)),
        out_shape=jax.ShapeDtypeStruct((b * s, w), BF16),
        scratch_shapes=[pltpu.VMEM((n_pairs, LANES, LANES), F32)],
        compiler_params=_params("parallel", "arbitrary"),
        name="rwkv_chunk",
    )(r, k, v, lw, kk, a, g, r_k.reshape(1, w), ln_w.reshape(1, w), ln_b.reshape(1, w))


def _hgrn_body(lbraw_ref, ng_ref, q_ref, f_ref, i_ref, g_ref, o_ref, st_ref, *, block, layer, group):
    @pl.when(pl.program_id(2) == 0)
    def _():
        st_ref[...] = jnp.zeros_like(st_ref)

    lbraw = lbraw_ref[...]
    e = jnp.exp(lbraw - jnp.max(lbraw, axis=0, keepdims=True))
    soft = e / jnp.sum(e, axis=0, keepdims=True)
    lb_all = jnp.sum(soft[:layer + 1], axis=0, keepdims=True) - soft[0:1]
    ng = ng_ref[...]

    c, sub = HG_CHUNK, HG_SUB
    n_sub = c // sub
    row = lax.broadcasted_iota(jnp.int32, (c, c), 0)
    col = lax.broadcasted_iota(jnp.int32, (c, c), 1)
    ones_lower = jnp.where(col <= row, 1.0, 0.0).astype(BF16)
    ones_sq = jnp.ones((LANES, LANES), BF16)
    ri = lax.broadcasted_iota(jnp.int32, (c, 1), 0)
    rs = lax.broadcasted_iota(jnp.int32, (sub, 1), 0)
    zeros_sub = jnp.zeros((sub, LANES), F32)
    off_row = lax.broadcasted_iota(jnp.int32, (c, (n_sub - 1) * c), 0) // sub
    off_col = lax.broadcasted_iota(jnp.int32, (c, (n_sub - 1) * c), 1) // c
    off_keep = off_row == off_col + 1
    lanes = [slice(h * LANES, (h + 1) * LANES) for h in range(group)]
    lb = [lb_all[:, ln] for ln in lanes]

    def each(fn, *lists):
        return [fn(*args) for args in zip(*lists)]

    def diag_pieces(q, k, b):
        pieces = []
        for i in range(n_sub):
            sl = slice(sub * i, sub * (i + 1))
            qi, ki, bi = q[sl], k[sl], b[sl]
            for s in range(sub):
                decay = jnp.exp2(jnp.where(rs >= s, bi - bi[s:s + 1], NEG_BIG))
                pieces.append(qi * decay * ki[s:s + 1])
        return jnp.concatenate(pieces, axis=0)

    def diag_apply(sums, iv):
        parts = []
        for i in range(n_sub):
            od = zeros_sub
            for s in range(sub):
                r0 = (i * sub + s) * sub
                od = od + sums[r0:r0 + sub] * iv[sub * i + s:sub * i + s + 1]
            parts.append(od)
        return jnp.concatenate(parts, axis=0)

    def chunk_step(ci, carry):
        rows = pl.ds(pl.multiple_of(ci * c, c), c)
        q = [q_ref[rows, ln] for ln in lanes]
        iv = [i_ref[rows, ln] for ln in lanes]
        fg = each(lambda ln, lb_: lb_ + (1.0 - lb_) * _sigmoid(f_ref[rows, ln]), lanes, lb)
        k = each(lambda t: 1.0 - t, fg)
        b = each(lambda t: _cumsum_rows(ones_lower, jnp.log(t)) * LOG2_E, fg)
        st = [st_ref[h] for h in range(group)]
        o_inter = each(lambda q_, b_, s_: _dot_bf16(q_ * jnp.exp2(b_), s_, _NT), q, b, st)

        def off_scores(q_, k_, b_):
            brefs = [b_[sub * i - 1:sub * i] for i in range(1, n_sub)]
            bref_rows = jnp.concatenate([zeros_sub] + [jnp.broadcast_to(t, (sub, LANES)) for t in brefs], axis=0)
            qh = q_ * jnp.exp2(jnp.where(ri >= sub, b_ - bref_rows, NEG_BIG))
            kh = jnp.concatenate([k_ * jnp.exp2(jnp.where(ri < sub * (i + 1), t - b_, NEG_BIG))
                                  for i, t in enumerate(brefs)], axis=0)
            return _dot_bf16(qh, kh, _NT)
        scores = each(off_scores, q, k, b)

        stacked = each(diag_pieces, q, k, b)
        sums = each(lambda t: jnp.dot(t.astype(BF16), ones_sq, preferred_element_type=F32), stacked)
        ivb = each(lambda t: t.astype(BF16), iv)
        o_off = each(lambda sc, v_: jnp.dot(jnp.where(off_keep, sc, 0.0).astype(BF16),
                                            jnp.concatenate([v_] * (n_sub - 1), axis=0),
                                            preferred_element_type=F32), scores, ivb)
        b_end = each(lambda t: t[c - 1:c], b)
        upd = each(lambda v_, k_, b_, e_: lax.dot_general(v_, (k_ * jnp.exp2(e_ - b_)).astype(BF16), _TN,
                                                          preferred_element_type=F32), ivb, k, b, b_end)
        for h in range(group):
            st_ref[h] = st[h] * jnp.exp2(b_end[h]) + upd[h]
            o = o_inter[h] + o_off[h] + diag_apply(sums[h], iv[h])
            o = o * lax.rsqrt(jnp.mean(o * o, axis=-1, keepdims=True) + EPS) * ng
            gg = g_ref[rows, lanes[h]]
            o_ref[rows, lanes[h]] = (o * (gg * _sigmoid(gg))).astype(o_ref.dtype)
        return carry

    lax.fori_loop(0, block // c, chunk_step, 0, unroll=8)


def _hgrn2(p, lb_raw, norm_g, batch, seq, layer, block=512, group=4):
    block = min(block, seq)
    nb = seq // block
    depth = lb_raw.shape[0]
    n_groups = HG_HEADS // group
    width = group * LANES
    col = lambda part: pl.BlockSpec((block, width), lambda b, h, i: (b * nb + i, part * n_groups + h))
    return pl.pallas_call(
        functools.partial(_hgrn_body, block=block, layer=layer, group=group),
        grid=(batch, n_groups, nb),
        in_specs=[pl.BlockSpec((depth, width), lambda b, h, i: (0, h)),
                  pl.BlockSpec((1, LANES), lambda b, h, i: (0, 0)),
                  col(0), col(1), col(2), col(3)],
        out_specs=pl.BlockSpec((block, width), lambda b, h, i: (b * nb + i, h)),
        out_shape=jax.ShapeDtypeStruct((batch * seq, D_MODEL), BF16),
        scratch_shapes=[pltpu.VMEM((group, LANES, LANES), F32)],
        compiler_params=_params("parallel", "parallel", "arbitrary"),
        name="hgrn2",
    )(lb_raw, norm_g.reshape(1, LANES), p, p, p, p)


def _ffn(x, h, layer, w_gate, w_up, w_down, next_g, last):
    hidden = _ffn_up(h, w_gate, w_up, layer)
    return _matmul_residual_norm([hidden], w_down, layer, x, next_g, F32 if last else BF16, emit_x=not last, tm=256)


def _even_mixer(x, h, batch, seq, layer, j, w_in, w_out, ffn_g, lam_vec, subln_g, mu, w0, w_up, a0, a_up, g_up,
                k_k, k_a, r_k, ln_w, ln_b):
    qk = _qk_proj(h, w_in, j, seq)
    v = _matmul(h, w_in, j, 2 * DA_WIDTH, DA_WIDTH, BF16, tm=2048)
    lam_init = 0.8 - 0.6 * math.exp(-0.3 * layer)
    a_out = _diff_attention(qk, v, lam_vec, subln_g, batch, seq, lam_init)

    rw0 = 3 * DA_WIDTH
    prw = _matmul(h, w_in, j, rw0, 3 * RW_WIDTH, F32, tm=2048)
    w_lora = jnp.pad(w_in[j, :, rw0 + 3 * RW_WIDTH:], ((0, 0), (0, RW_LORA_PAD - RW_LORA)))
    plora = _matmul(h, w_lora[None], 0, 0, RW_LORA_PAD, F32)
    tok = _rwkv_prep(prw.reshape(batch, seq, -1), plora.reshape(batch, seq, -1), mu, w0, w_up, a0, a_up, g_up,
                     k_k, k_a)
    b_out = _rwkv_chunked(*tok, r_k, ln_w, ln_b)
    return _matmul_residual_norm([a_out, b_out], w_out, j, x, ffn_g, BF16)


def _odd_mixer(x, h, batch, seq, layer, j, w_in, w_out, ffn_g, lb_raw, hg_norm_g):
    p = _matmul(h, w_in, j, 0, w_in.shape[-1], F32, tm=2048)
    o = _hgrn2(p, lb_raw, hg_norm_g, batch, seq, layer)
    return _matmul_residual_norm([o], w_out, j, x, ffn_g, BF16)


def kernel(x, attn_norm_g, ffn_norm_g, final_norm_g, even_w_in, even_w_out, da_lambda, da_subln_g, rw_mu, rw_w0, rw_w_up, rw_a0, rw_a_up, rw_g_up, rw_k_k, rw_k_a, rw_r_k, rw_ln_w, rw_ln_b, odd_w_in, odd_w_out, hg_lower_bound, hg_norm_g, ffn_w_gate, ffn_w_up, ffn_w_down):
    batch, seq, d = x.shape
    depth = attn_norm_g.shape[0]
    xs = x.reshape(batch * seq, d)
    w_down, w_out_even, w_out_odd = (t.astype(BF16) for t in (ffn_w_down, even_w_out, odd_w_out))
    h = _rmsnorm(xs, attn_norm_g[0], BF16)
    for layer in range(depth):
        j = layer // 2
        if layer % 2 == 0:
            xs, h = _even_mixer(xs, h, batch, seq, layer, j, even_w_in, w_out_even, ffn_norm_g[layer], da_lambda[j],
                                da_subln_g[j], rw_mu[j], rw_w0[j], rw_w_up[j], rw_a0[j], rw_a_up[j], rw_g_up[j],
                                rw_k_k[j], rw_k_a[j], rw_r_k[j], rw_ln_w[j], rw_ln_b[j])
        else:
            xs, h = _odd_mixer(xs, h, batch, seq, layer, j, odd_w_in, w_out_odd, ffn_norm_g[layer],
                               hg_lower_bound, hg_norm_g[j])
        last = layer == depth - 1
        next_g = final_norm_g if last else attn_norm_g[layer + 1]
        xs, h = _ffn(xs, h, layer, ffn_w_gate, ffn_w_up, w_down, next_g, last)
    return h.reshape(batch, seq, d)
```
